```python
import math
import jax
import jax.numpy as jnp
from jax import lax
import numpy as np

D_MODEL = 2048
BATCH = 4
SEQ = 4096
DEPTH = 2

MEM_LEN = 256
LN_EPS = 1e-5
DEEPNORM_ALPHA = (2 * DEPTH) ** 0.25
DEEPNORM_BETA = (8 * DEPTH) ** -0.25

RG_WIDTH = D_MODEL // 4
RG_BLOCKS = 4
RG_CONV = 4
RG_C = 8.0

DN_WIDTH = D_MODEL // 2
DN_HEAD_DIM = 128
DN_HEADS = DN_WIDTH // DN_HEAD_DIM
DN_CONV = 4
DN_CHUNK = 64

NSA_WIDTH = D_MODEL // 4
NSA_HEAD_DIM = 64
NSA_HEADS = NSA_WIDTH // NSA_HEAD_DIM
NSA_KV_GROUPS = 2
NSA_KV_WIDTH = NSA_KV_GROUPS * NSA_HEAD_DIM
NSA_CMP_BLOCK = 32
NSA_CMP_STRIDE = 16
NSA_CMP_HIDDEN = 256
NSA_SEL_BLOCK = 64
NSA_TOP_BLOCKS = 16
NSA_WINDOW = 512
Q_BLOCK = 128
FORCE_SCORE = 1e4

REL_BUCKETS = 32
REL_MAX_DIST = 1024

XA_HEADS = 4
XA_HEAD_DIM = 128
XA_WIDTH = XA_HEADS * XA_HEAD_DIM

N_EXPERTS = 64
TOP_K = 8
N_GROUPS = 8
TOPK_GROUPS = 4
D_EXPERT = 512
ROUTED_SCALE = 2.5
MOE_BLOCK = 128

D_IN = 2 * RG_WIDTH + 4 * DN_WIDTH + 2 * DN_HEADS + NSA_WIDTH + 6 * NSA_KV_WIDTH + 3 * NSA_HEADS
NEG = -1e30

kernel_name = 'hybrid_rglru_deltanet_nsa_moe_deepnorm'


def _split_points():
    sizes = (RG_WIDTH, RG_WIDTH,
             DN_WIDTH, DN_WIDTH, DN_WIDTH, DN_WIDTH, DN_HEADS, DN_HEADS,
             NSA_WIDTH, 6 * NSA_KV_WIDTH, 3 * NSA_HEADS)
    pts, acc = [], 0
    for s in sizes[:-1]:
        acc += s
        pts.append(acc)
    return pts


def layer_norm(x, g, b):
    xf = x.astype(jnp.float32)
    xc = xf - jnp.mean(xf, -1, keepdims=True)
    var = jnp.mean(xc * xc, -1, keepdims=True)
    return (xc * lax.rsqrt(var + LN_EPS) * g + b).astype(x.dtype)


def masked_softmax(logits, mask):
    logits = jnp.where(mask, logits.astype(jnp.float32), NEG)
    m = jnp.max(logits, -1, keepdims=True)
    e = jnp.where(mask, jnp.exp(logits - m), 0.0)
    return e / jnp.maximum(jnp.sum(e, -1, keepdims=True), 1e-30)


def l2_normalize(x):
    return x * lax.rsqrt(jnp.sum(x * x, -1, keepdims=True) + 1e-6)


def causal_dwconv(x, w):
    k = w.shape[0]
    return lax.conv_general_dilated(x, w.astype(x.dtype)[:, None, :], (1,), [(k - 1, 0)],
                                    dimension_numbers=('NWC', 'WIO', 'NWC'),
                                    feature_group_count=x.shape[-1])


def rel_bucket(dist):
    n = jnp.maximum(dist, 0)
    exact = REL_BUCKETS // 2
    big = exact + (jnp.log(jnp.maximum(n, 1).astype(jnp.float32) / exact)
                   / math.log(REL_MAX_DIST / exact) * (REL_BUCKETS - exact)).astype(jnp.int32)
    return jnp.where(n < exact, n, jnp.minimum(big, REL_BUCKETS - 1))


def _linear_combine(c1, c2):
    a1, b1 = c1
    a2, b2 = c2
    return a1 * a2, a2 * b1 + b2


def rglru_block(xa, ga, conv_w, conv_b, wa, ba, wi, bi, lam):
    bsz, t_len, c = xa.shape
    xc = causal_dwconv(xa.astype(jnp.float32), conv_w) + conv_b
    xg = xc.reshape(bsz, t_len, RG_BLOCKS, c // RG_BLOCKS)
    r = jax.nn.sigmoid(jnp.einsum('btnc,ncd->btnd', xg, wa).reshape(bsz, t_len, c) + ba)
    i = jax.nn.sigmoid(jnp.einsum('btnc,ncd->btnd', xg, wi).reshape(bsz, t_len, c) + bi)
    log_a = -RG_C * r * jax.nn.softplus(-lam.astype(jnp.float32))
    a = jnp.exp(log_a)
    b = jnp.sqrt(-jnp.expm1(2.0 * log_a)) * (i * xc)
    _, h = lax.associative_scan(_linear_combine, (a, b), axis=1)
    return h * jax.nn.gelu(ga.astype(jnp.float32))


def chunk_gated_delta_rule(q, k, v, g, beta):
    bsz, t_len, h, dk = q.shape
    c = DN_CHUNK
    n = t_len // c

    def chunks(u):
        return u.reshape(bsz, n, c, h, -1).transpose(0, 3, 1, 2, 4)

    q = chunks(q * dk ** -0.5)
    k = chunks(k)
    v = chunks(v)
    beta = beta.reshape(bsz, n, c, h).transpose(0, 3, 1, 2)
    gc = jnp.cumsum(g.reshape(bsz, n, c, h).transpose(0, 3, 1, 2), axis=-1)
    idx = jnp.arange(c)
    incl = idx[:, None] >= idx[None, :]
    strict = idx[:, None] > idx[None, :]
    diff = gc[..., :, None] - gc[..., None, :]
    decay = jnp.where(incl, jnp.exp(jnp.where(incl, diff, 0.0)), 0.0)
    kb = k * beta[..., None]
    lower = jnp.where(strict, jnp.einsum('bhncd,bhnjd->bhncj', kb, k) * decay, 0.0)
    eye = jnp.eye(c, dtype=lower.dtype)
    t_inv = lax.linalg.triangular_solve(lower + eye, jnp.broadcast_to(eye, lower.shape),
                                        left_side=True, lower=True, unit_diagonal=True)
    u = t_inv @ (v * beta[..., None])
    w = t_inv @ (kb * jnp.exp(gc)[..., None])
    attn = jnp.einsum('bhncd,bhnjd->bhncj', q, k) * decay
    qg = q * jnp.exp(gc)[..., None]
    kg = k * jnp.exp(gc[..., -1:] - gc)[..., None]
    g_last = jnp.exp(gc[..., -1])

    def step(state, xs):
        u_n, w_n, attn_n, qg_n, kg_n, gl_n = xs
        v_new = u_n - jnp.einsum('bhck,bhkv->bhcv', w_n, state)
        o_n = jnp.einsum('bhck,bhkv->bhcv', qg_n, state) + jnp.einsum('bhcj,bhjv->bhcv', attn_n, v_new)
        state = state * gl_n[..., None, None] + jnp.einsum('bhck,bhcv->bhkv', kg_n, v_new)
        return state, o_n

    xs = tuple(jnp.moveaxis(u_, 2, 0) for u_ in (u, w, attn, qg, kg, g_last))
    s0 = jnp.zeros((bsz, h, dk, v.shape[-1]), jnp.float32)
    _, o = lax.scan(step, s0, xs)
    return o.transpose(1, 0, 3, 2, 4).reshape(bsz, t_len, h, -1)


def gated_deltanet(q, k, v, z, a, b, conv_w, a_log, dt_bias, norm_w):
    bsz, t_len, _ = q.shape
    f32 = jnp.float32
    qkv = jax.nn.silu(causal_dwconv(jnp.concatenate([q, k, v], -1).astype(f32), conv_w))
    q, k, v = [u.reshape(bsz, t_len, DN_HEADS, DN_HEAD_DIM) for u in jnp.split(qkv, 3, axis=-1)]
    q = l2_normalize(q)
    k = l2_normalize(k)
    beta = jax.nn.sigmoid(b.astype(f32))
    g = -jnp.exp(a_log.astype(f32)) * jax.nn.softplus(a.astype(f32) + dt_bias)
    o = chunk_gated_delta_rule(q, k, v, g, beta)
    o = o * lax.rsqrt(jnp.mean(o * o, -1, keepdims=True) + 1e-6) * norm_w
    o = o * jax.nn.silu(z.astype(f32).reshape(bsz, t_len, DN_HEADS, DN_HEAD_DIM))
    return o.reshape(bsz, t_len, DN_WIDTH)


def nsa_compress(kraw, pe, w1, w2):
    bsz, t_len, g, d = kraw.shape
    n_cmp = (t_len - NSA_CMP_BLOCK) // NSA_CMP_STRIDE + 1
    idx = jnp.arange(n_cmp)[:, None] * NSA_CMP_STRIDE + jnp.arange(NSA_CMP_BLOCK)[None, :]
    blocks = kraw[:, idx] + pe[None, None, :, None, :]
    flat = blocks.transpose(0, 1, 3, 2, 4).reshape(bsz, n_cmp, g, NSA_CMP_BLOCK * d)
    return jax.nn.gelu(flat @ w1) @ w2


def nsa_attention(q, kv, gate_logits, rel_bias, pe_k, pe_v, phi_k1, phi_k2, phi_v1, phi_v2):
    bsz, t_len, _ = q.shape
    h, g, d = NSA_HEADS, NSA_KV_GROUPS, NSA_HEAD_DIM
    hpg = h // g
    f32 = jnp.float32
    q = (q.astype(f32) * d ** -0.5).reshape(bsz, t_len, h, d)
    kc, vc, ks, vs, kw, vw = [u.reshape(bsz, t_len, g, d) for u in jnp.split(kv.astype(f32), 6, axis=-1)]
    gates = jax.nn.sigmoid(gate_logits.astype(f32)).reshape(bsz, t_len, h, 3)
    table = rel_bias.astype(f32).T.reshape(g, hpg, REL_BUCKETS)

    k_cmp = nsa_compress(kc, pe_k, phi_k1, phi_k2)
    v_cmp = nsa_compress(vc, pe_v, phi_v1, phi_v2)
    n_cmp = k_cmp.shape[1]
    cmp_start = jnp.arange(n_cmp) * NSA_CMP_STRIDE
    cmp_end = cmp_start + NSA_CMP_BLOCK - 1

    n_sel = t_len // NSA_SEL_BLOCK
    n_top = min(NSA_TOP_BLOCKS, n_sel)
    sel_start = jnp.arange(n_sel) * NSA_SEL_BLOCK
    overlap = ((cmp_start[:, None] < sel_start[None, :] + NSA_SEL_BLOCK)
               & (cmp_end[:, None] >= sel_start[None, :])).astype(f32)
    ks_blk = ks.reshape(bsz, n_sel, NSA_SEL_BLOCK, g, d).transpose(0, 3, 1, 2, 4)
    vs_blk = vs.reshape(bsz, n_sel, NSA_SEL_BLOCK, g, d).transpose(0, 3, 1, 2, 4)
    gather_blocks = jax.vmap(jax.vmap(lambda blk, ix: blk[ix]))
    sel_off = jnp.arange(NSA_SEL_BLOCK)
    j_sel = jnp.arange(n_sel)
    g_ix = jnp.arange(g)[None, :, None, None, None]
    h_ix = jnp.arange(hpg)[None, None, :, None, None]

    pad = jnp.zeros((bsz, NSA_WINDOW, g, d), f32)
    kw_pad = jnp.concatenate([pad, kw], axis=1)
    vw_pad = jnp.concatenate([pad, vw], axis=1)
    band = Q_BLOCK + NSA_WINDOW
    qi = jnp.arange(Q_BLOCK)
    band_dist = qi[:, None] + NSA_WINDOW - jnp.arange(band)[None, :]
    band_bias = table[:, :, rel_bucket(band_dist)]
    band_ok = (band_dist >= 0) & (band_dist < NSA_WINDOW)

    def one_block(s):
        t = s + qi
        qb = lax.dynamic_slice_in_dim(q, s, Q_BLOCK, axis=1).reshape(bsz, Q_BLOCK, g, hpg, d).transpose(0, 2, 3, 1, 4)
        gb = lax.dynamic_slice_in_dim(gates, s, Q_BLOCK, axis=1).reshape(bsz, Q_BLOCK, g, hpg, 3).transpose(0, 2, 3, 1, 4)
        c_logit = jnp.einsum('bghqd,bngd->bghqn', qb, k_cmp) + table[:, :, rel_bucket(t[:, None] - cmp_end[None, :])]
        p_cmp = masked_softmax(c_logit, cmp_end[None, :] <= t[:, None])
        o_cmp = jnp.einsum('bghqn,bngd->bghqd', p_cmp, v_cmp)
        importance = jnp.einsum('bghqn,nj->bgqj', p_cmp, overlap)
        q_blk = (t // NSA_SEL_BLOCK)[:, None]
        forced = (j_sel[None, :] == 0) | (j_sel[None, :] == q_blk) | (j_sel[None, :] == q_blk - 1)
        score = jnp.where(sel_start[None, :] <= t[:, None], jnp.where(forced, FORCE_SCORE, importance), -FORCE_SCORE)
        _, top = lax.top_k(score, n_top)
        flat_top = top.reshape(bsz, g, Q_BLOCK * n_top)
        k_sel = gather_blocks(ks_blk, flat_top).reshape(bsz, g, Q_BLOCK, n_top * NSA_SEL_BLOCK, d)
        v_sel = gather_blocks(vs_blk, flat_top).reshape(bsz, g, Q_BLOCK, n_top * NSA_SEL_BLOCK, d)
        pos = (top[..., None] * NSA_SEL_BLOCK + sel_off).reshape(bsz, g, Q_BLOCK, n_top * NSA_SEL_BLOCK)
        s_dist = t[:, None] - pos
        s_logit = jnp.einsum('bghqd,bgqkd->bghqk', qb, k_sel) + table[g_ix, h_ix, rel_bucket(s_dist)[:, :, None]]
        p_sel = masked_softmax(s_logit, (s_dist >= 0)[:, :, None])
        o_sel = jnp.einsum('bghqk,bgqkd->bghqd', p_sel, v_sel)
        kwb = lax.dynamic_slice_in_dim(kw_pad, s, band, axis=1)
        vwb = lax.dynamic_slice_in_dim(vw_pad, s, band, axis=1)
        w_ok = band_ok & ((s - NSA_WINDOW + jnp.arange(band)) >= 0)[None, :]
        w_logit = jnp.einsum('bghqd,bkgd->bghqk', qb, kwb) + band_bias
        p_win = masked_softmax(w_logit, w_ok)
        o_win = jnp.einsum('bghqk,bkgd->bghqd', p_win, vwb)
        o = gb[..., 0:1] * o_cmp + gb[..., 1:2] * o_sel + gb[..., 2:3] * o_win
        return o.transpose(0, 3, 1, 2, 4).reshape(bsz, Q_BLOCK, h * d)

    out = lax.map(one_block, jnp.arange(t_len // Q_BLOCK) * Q_BLOCK)
    return out.transpose(1, 0, 2, 3).reshape(bsz, t_len, h * d)


def memory_cross_attention(x, mem, wq, wk, wv, wo):
    bsz, t_len, _ = x.shape
    m_len = mem.shape[1]
    q = (x @ wq).reshape(bsz, t_len, XA_HEADS, XA_HEAD_DIM)
    k = (mem @ wk).reshape(bsz, m_len, XA_HEADS, XA_HEAD_DIM)
    v = (mem @ wv).reshape(bsz, m_len, XA_HEADS, XA_HEAD_DIM)
    logits = jnp.einsum('bthd,bmhd->bhtm', q, k).astype(jnp.float32) * XA_HEAD_DIM ** -0.5
    p = jax.nn.softmax(logits, axis=-1)
    o = jnp.einsum('bhtm,bmhd->bthd', p, v.astype(jnp.float32)).reshape(bsz, t_len, XA_WIDTH)
    return o.astype(x.dtype) @ wo


def swiglu(x, w_gate, w_up, w_down):
    return (jax.nn.silu(x @ w_gate) * (x @ w_up)) @ w_down


def routed_experts(xf, top_e, gate, w_gate, w_up, w_down):
    n_tok = xf.shape[0]
    n_assign = n_tok * TOP_K
    n_rows = n_assign + N_EXPERTS * MOE_BLOCK
    n_blocks = n_rows // MOE_BLOCK
    flat_e = top_e.reshape(-1)
    flat_tok = jnp.arange(n_assign, dtype=jnp.int32) // TOP_K
    flat_w = gate.reshape(-1)
    order = jnp.argsort(flat_e)
    e_sorted = flat_e[order]
    counts = jnp.zeros(N_EXPERTS, jnp.int32).at[flat_e].add(1)
    padded = (counts + MOE_BLOCK - 1) // MOE_BLOCK * MOE_BLOCK
    starts = jnp.cumsum(counts) - counts
    pad_ends = jnp.cumsum(padded)
    pad_starts = pad_ends - padded
    dest = pad_starts[e_sorted] + jnp.arange(n_assign, dtype=jnp.int32) - starts[e_sorted]
    row_tok = jnp.zeros(n_rows, jnp.int32).at[dest].set(flat_tok[order])
    row_w = jnp.zeros(n_rows, jnp.float32).at[dest].set(flat_w[order])
    block_e = jnp.minimum(jnp.searchsorted(pad_ends, jnp.arange(n_blocks, dtype=jnp.int32) * MOE_BLOCK, side='right'),
                          N_EXPERTS - 1)

    def body(y, blk):
        tok, wt, e = blk
        xb = xf[tok]
        hb = jax.nn.silu(xb @ w_gate[e]) * (xb @ w_up[e])
        return y.at[tok].add((hb @ w_down[e]).astype(jnp.float32) * wt[:, None]), None

    y0 = jnp.zeros(xf.shape, jnp.float32)
    y, _ = lax.scan(body, y0, (row_tok.reshape(n_blocks, MOE_BLOCK), row_w.reshape(n_blocks, MOE_BLOCK), block_e))
    return y


def moe_ffn(x, w_router, router_bias, w_gate, w_up, w_down, ws_gate, ws_up, ws_down):
    bsz, t_len, d = x.shape
    xf = x.reshape(-1, d)
    n_tok = xf.shape[0]
    scores = jax.nn.sigmoid((xf @ w_router).astype(jnp.float32))
    choice = scores + router_bias.astype(jnp.float32)
    grp = choice.reshape(n_tok, N_GROUPS, N_EXPERTS // N_GROUPS)
    grp_score = jnp.sum(lax.top_k(grp, 2)[0], axis=-1)
    _, grp_idx = lax.top_k(grp_score, TOPK_GROUPS)
    grp_mask = jnp.any(grp_idx[..., None] == jnp.arange(N_GROUPS), axis=-2)
    expert_mask = jnp.repeat(grp_mask, N_EXPERTS // N_GROUPS, axis=-1)
    _, top_e = lax.top_k(jnp.where(expert_mask, choice, -1e9), TOP_K)
    top_s = jnp.take_along_axis(scores, top_e, axis=-1)
    gate = top_s / jnp.sum(top_s, -1, keepdims=True) * ROUTED_SCALE
    routed = routed_experts(xf, top_e, gate, w_gate, w_up, w_down)
    shared = swiglu(xf, ws_gate, ws_up, ws_down).astype(jnp.float32)
    return (routed + shared).astype(x.dtype).reshape(bsz, t_len, d)


def setup_inputs(seed: int = 0) -> dict:
    key = jax.random.key(seed)
    keys = iter(jax.random.split(key, 48))
    f32 = jnp.float32
    L, D = DEPTH, D_MODEL

    def normal(shape, scale):
        return jax.random.normal(next(keys), shape, f32) * scale

    def uniform(shape, lo, hi):
        return jax.random.uniform(next(keys), shape, f32, lo, hi)

    p = uniform((L, RG_WIDTH), 0.9, 0.999) ** (1.0 / RG_C)
    rg_lambda = jnp.log(p) - jnp.log1p(-p)
    dt = jnp.exp(uniform((L, DN_HEADS), math.log(1e-3), math.log(1e-1)))
    dn_dt_bias = dt + jnp.log(-jnp.expm1(-dt))
    rg_blk = RG_WIDTH // RG_BLOCKS
    cmp_in = NSA_CMP_BLOCK * NSA_HEAD_DIM
    beta = DEEPNORM_BETA
    return {
        'x': normal((BATCH, SEQ, D), 1.0),
        'mem': normal((BATCH, MEM_LEN, D), 1.0),
        'rel_bias': normal((REL_BUCKETS, NSA_HEADS), 0.1),
        'w_in': normal((L, D, D_IN), D ** -0.5),
        'rg_conv_w': normal((L, RG_CONV, RG_WIDTH), RG_CONV ** -0.5),
        'rg_conv_b': normal((L, RG_WIDTH), 0.01),
        'rg_wa': normal((L, RG_BLOCKS, rg_blk, rg_blk), rg_blk ** -0.5),
        'rg_ba': normal((L, RG_WIDTH), 0.01),
        'rg_wi': normal((L, RG_BLOCKS, rg_blk, rg_blk), rg_blk ** -0.5),
        'rg_bi': normal((L, RG_WIDTH), 0.01),
        'rg_lambda': rg_lambda,
        'dn_conv_w': normal((L, DN_CONV, 3 * DN_WIDTH), DN_CONV ** -0.5),
        'dn_a_log': jnp.log(uniform((L, DN_HEADS), 1.0, 16.0)),
        'dn_dt_bias': dn_dt_bias,
        'dn_norm_w': 1.0 + normal((L, DN_HEAD_DIM), 0.02),
        'nsa_pe_k': normal((L, NSA_CMP_BLOCK, NSA_HEAD_DIM), 0.1),
        'nsa_pe_v': normal((L, NSA_CMP_BLOCK, NSA_HEAD_DIM), 0.1),
        'nsa_phi_k1': normal((L, cmp_in, NSA_CMP_HIDDEN), cmp_in ** -0.5),
        'nsa_phi_k2': normal((L, NSA_CMP_HIDDEN, NSA_HEAD_DIM), NSA_CMP_HIDDEN ** -0.5),
        'nsa_phi_v1': normal((L, cmp_in, NSA_CMP_HIDDEN), cmp_in ** -0.5),
        'nsa_phi_v2': normal((L, NSA_CMP_HIDDEN, NSA_HEAD_DIM), NSA_CMP_HIDDEN ** -0.5),
        'w_out': normal((L, RG_WIDTH + DN_WIDTH + NSA_WIDTH, D), (RG_WIDTH + DN_WIDTH + NSA_WIDTH) ** -0.5 * beta),
        'ln1_g': 1.0 + normal((L, D), 0.02),
        'ln1_b': normal((L, D), 0.02),
        'xa_wq': normal((L, D, XA_WIDTH), D ** -0.5),
        'xa_wk': normal((L, D, XA_WIDTH), D ** -0.5),
        'xa_wv': normal((L, D, XA_WIDTH), D ** -0.5),
        'xa_wo': normal((L, XA_WIDTH, D), XA_WIDTH ** -0.5 * beta),
        'ln2_g': 1.0 + normal((L, D), 0.02),
        'ln2_b': normal((L, D), 0.02),
        'moe_router': normal((L, D, N_EXPERTS), D ** -0.5),
        'moe_router_bias': normal((L, N_EXPERTS), 0.01),
        'moe_w_gate': normal((L, N_EXPERTS, D, D_EXPERT), D ** -0.5),
        'moe_w_up': normal((L, N_EXPERTS, D, D_EXPERT), D ** -0.5),
        'moe_w_down': normal((L, N_EXPERTS, D_EXPERT, D), D_EXPERT ** -0.5 * beta),
        'shared_w_gate': normal((L, D, D_EXPERT), D ** -0.5),
        'shared_w_up': normal((L, D, D_EXPERT), D ** -0.5),
        'shared_w_down': normal((L, D_EXPERT, D), D_EXPERT ** -0.5 * beta),
        'ln3_g': 1.0 + normal((L, D), 0.02),
        'ln3_b': normal((L, D), 0.02),
    }


def reference(x, mem, rel_bias, w_in, rg_conv_w, rg_conv_b, rg_wa, rg_ba, rg_wi, rg_bi, rg_lambda,
              dn_conv_w, dn_a_log, dn_dt_bias, dn_norm_w, nsa_pe_k, nsa_pe_v, nsa_phi_k1, nsa_phi_k2,
              nsa_phi_v1, nsa_phi_v2, w_out, ln1_g, ln1_b, xa_wq, xa_wk, xa_wv, xa_wo, ln2_g, ln2_b,
              moe_router, moe_router_bias, moe_w_gate, moe_w_up, moe_w_down, shared_w_gate, shared_w_up,
              shared_w_down, ln3_g, ln3_b):
    split_pts = _split_points()
    for l in range(DEPTH):
        h = x @ w_in[l]
        (rg_x, rg_g, dn_q, dn_k, dn_v, dn_z, dn_a, dn_b,
         nsa_q, nsa_kv, nsa_g) = jnp.split(h, split_pts, axis=-1)
        y_a = rglru_block(rg_x, rg_g, rg_conv_w[l], rg_conv_b[l], rg_wa[l], rg_ba[l], rg_wi[l], rg_bi[l], rg_lambda[l])
        y_b = gated_deltanet(dn_q, dn_k, dn_v, dn_z, dn_a, dn_b, dn_conv_w[l], dn_a_log[l], dn_dt_bias[l], dn_norm_w[l])
        y_c = nsa_attention(nsa_q, nsa_kv, nsa_g, rel_bias, nsa_pe_k[l], nsa_pe_v[l],
                            nsa_phi_k1[l], nsa_phi_k2[l], nsa_phi_v1[l], nsa_phi_v2[l])
        mix = jnp.concatenate([y_a, y_b, y_c], axis=-1).astype(x.dtype) @ w_out[l]
        x = layer_norm(DEEPNORM_ALPHA * x + mix, ln1_g[l], ln1_b[l])
        xa = memory_cross_attention(x, mem, xa_wq[l], xa_wk[l], xa_wv[l], xa_wo[l])
        x = layer_norm(DEEPNORM_ALPHA * x + xa, ln2_g[l], ln2_b[l])
        ff = moe_ffn(x, moe_router[l], moe_router_bias[l], moe_w_gate[l], moe_w_up[l], moe_w_down[l],
                     shared_w_gate[l], shared_w_up[l], shared_w_down[l])
        x = layer_norm(DEEPNORM_ALPHA * x + ff, ln3_g[l], ln3_b[l])
    return x
```

```python
import functools
import math

import jax
import jax.numpy as jnp
from jax import lax
from jax.experimental import pallas as pl
from jax.experimental.pallas import tpu as pltpu

F32 = jnp.float32
BF16 = jnp.bfloat16

LN_EPS = 1e-5
NEG = -1e30

RG_BLOCKS = 4
RG_CONV = 4
RG_C = 8.0
DN_HEAD_DIM = 128
DN_CONV = 4
DN_CHUNK = 64
NSA_HEAD_DIM = 64
NSA_KV_GROUPS = 2
NSA_CMP_BLOCK = 32
NSA_CMP_STRIDE = 16
NSA_SEL_BLOCK = 64
NSA_TOP_BLOCKS = 16
NSA_WINDOW = 512
FORCE_SCORE = 1e4
REL_BUCKETS = 32
REL_MAX_DIST = 1024
XA_HEADS = 4
XA_HEAD_DIM = 128
TOP_K = 8
N_GROUPS = 8
TOPK_GROUPS = 4
ROUTED_SCALE = 2.5
MOE_ROWS = 256

V7X_LANES = 128
V7X_SUBLANES = 8
V7X_VMEM_LIMIT = 56 * 1024 * 1024


def _params(semantics, vmem=V7X_VMEM_LIMIT):
    return pltpu.CompilerParams(dimension_semantics=semantics, vmem_limit_bytes=vmem)


def _sigmoid(x):
    return 1.0 / (1.0 + jnp.exp(-x))


def _softplus(x):
    return jnp.maximum(x, 0.0) + jnp.log1p(jnp.exp(-jnp.abs(x)))


def _silu(x):
    return x * _sigmoid(x)


def _gelu_tanh(x):
    c = math.sqrt(2.0 / math.pi)
    return x * (0.5 * (1.0 + jnp.tanh(c * (x + 0.044715 * (x * x * x)))))


def _layer_norm(v, g, b):
    mu = jnp.mean(v, axis=-1, keepdims=True)
    vc = v - mu
    var = jnp.mean(vc * vc, axis=-1, keepdims=True)
    return vc * lax.rsqrt(var + LN_EPS) * g + b


def _dot(a, b):
    return jnp.dot(a, b, preferred_element_type=F32)


def _dot_nt(a, b):
    return lax.dot_general(a, b, (((1,), (1,)), ((), ())), preferred_element_type=F32)


def _dot_tn(a, b):
    return lax.dot_general(a, b, (((0,), (0,)), ((), ())), preferred_element_type=F32)


def _dot_hi(a, b):
    return jnp.dot(a, b, preferred_element_type=F32, precision=lax.Precision.HIGHEST)


def _inproj_kernel(x_ref, w_ref, wt_ref, o_ref, ot_ref, xb_ref):
    @pl.when(pl.program_id(1) == 0)
    def _():
        xb_ref[...] = x_ref[...].astype(BF16)
        ot_ref[...] = _dot_nt(wt_ref[...], xb_ref[...])

    o_ref[...] = _dot(xb_ref[...], w_ref[...])


def _inproj(x2d, w, wt, tm=512, tn=512):
    n, d = x2d.shape
    cols = w.shape[1]
    return pl.pallas_call(
        _inproj_kernel,
        grid=(n // tm, cols // tn),
        in_specs=[
            pl.BlockSpec((tm, d), lambda i, j: (i, 0)),
            pl.BlockSpec((d, tn), lambda i, j: (0, j)),
            pl.BlockSpec((wt.shape[0], d), lambda i, j: (0, 0)),
        ],
        out_specs=[
            pl.BlockSpec((tm, tn), lambda i, j: (i, j)),
            pl.BlockSpec((wt.shape[0], tm), lambda i, j: (0, i)),
        ],
        out_shape=[
            jax.ShapeDtypeStruct((n, cols), F32),
            jax.ShapeDtypeStruct((wt.shape[0], n), F32),
        ],
        scratch_shapes=[pltpu.VMEM((tm, d), BF16)],
        compiler_params=_params(("arbitrary", "arbitrary")),
        name="inproj",
    )(x2d, w, wt)


def _shifted_rows(halo, x, s):
    cat = jnp.concatenate([halo, x], axis=0)
    return pltpu.roll(cat, s, 0)[V7X_SUBLANES:]


def _causal_conv(halo, x, w):
    k = w.shape[0]
    y = x * w[k - 1:k]
    for s in range(1, k):
        y = y + _shifted_rows(halo, x, s) * w[k - 1 - s:k - s]
    return y


def _rg_kernel(xh_ref, x_ref, g_ref, cw_ref, cb_ref, wa_ref, wi_ref, ba_ref, bi_ref, lam_ref,
               o_ref, h_ref):
    c = pl.program_id(1)
    x = x_ref[0]
    rows, width = x.shape
    blk = width // RG_BLOCKS
    halo = jnp.where(c > 0, xh_ref[0], 0.0)
    xc = _causal_conv(halo, x, cw_ref[...]) + cb_ref[...]
    xcb = xc.astype(BF16)
    r_parts, i_parts = [], []
    for n in range(RG_BLOCKS):
        xg = xcb[:, n * blk:(n + 1) * blk]
        r_parts.append(_dot(xg, wa_ref[n]))
        i_parts.append(_dot(xg, wi_ref[n]))
    r = _sigmoid(jnp.concatenate(r_parts, axis=1) + ba_ref[...])
    gi = _sigmoid(jnp.concatenate(i_parts, axis=1) + bi_ref[...])
    log_a = -RG_C * r * _softplus(-lam_ref[...])
    a = jnp.exp(log_a)
    th = jnp.tanh(log_a)
    b = jnp.sqrt(-2.0 * th / (1.0 - th)) * (gi * xc)
    row = lax.broadcasted_iota(jnp.int32, (rows, 1), 0)
    s = 1
    while s < rows:
        a_s = pltpu.roll(a, s, 0)
        b_s = pltpu.roll(b, s, 0)
        m = row >= s
        b = jnp.where(m, a * b_s + b, b)
        a = jnp.where(m, a * a_s, a)
        s *= 2

    @pl.when(c == 0)
    def _():
        h_ref[...] = jnp.zeros_like(h_ref)

    h = a * h_ref[0:1] + b
    h_ref[...] = jnp.broadcast_to(h[rows - 1:rows], h_ref.shape)
    o_ref[0] = (h * _gelu_tanh(g_ref[0])).astype(o_ref.dtype)


def _rglru(h3, cw, cb, wa, wi, ba, bi, lam, width, rows=512):
    bsz, t_len, _ = h3.shape
    rows = min(rows, t_len)
    hb = rows // V7X_SUBLANES
    vec = lambda v: v.reshape(1, width)
    full = lambda shape: pl.BlockSpec(shape, lambda b, c: (0,) * len(shape))
    return pl.pallas_call(
        _rg_kernel,
        grid=(bsz, t_len // rows),
        in_specs=[
            pl.BlockSpec((1, V7X_SUBLANES, width), lambda b, c: (b, jnp.maximum(c * hb - 1, 0), 0)),
            pl.BlockSpec((1, rows, width), lambda b, c: (b, c, 0)),
            pl.BlockSpec((1, rows, width), lambda b, c: (b, c, 1)),
            full((RG_CONV, width)), full((1, width)),
            full(wa.shape), full(wi.shape),
            full((1, width)), full((1, width)), full((1, width)),
        ],
        out_specs=pl.BlockSpec((1, rows, width), lambda b, c: (b, c, 0)),
        out_shape=jax.ShapeDtypeStruct((bsz, t_len, width), BF16),
        scratch_shapes=[pltpu.VMEM((V7X_SUBLANES, width), F32)],
        compiler_params=_params(("arbitrary", "arbitrary")),
        name="rglru",
    )(h3, h3, h3, cw, vec(cb), wa.astype(BF16), wi.astype(BF16), vec(ba), vec(bi), vec(lam))


def _segment_cumsum(v, axis, seg):
    pos = lax.broadcasted_iota(jnp.int32, v.shape, axis) % seg
    s = 1
    while s < seg:
        v = v + jnp.where(pos >= s, pltpu.roll(v, s, axis), 0.0)
        s *= 2
    return v


def _l2_normalize(v):
    return v * lax.rsqrt(jnp.sum(v * v, axis=-1, keepdims=True) + 1e-6)


def _dn_kernel(alog_ref, dtb_ref,
               qh_ref, q_ref, kh_ref, k_ref, vh_ref, v_ref, z_ref, sm_ref, smt_ref,
               cwq_ref, cwk_ref, cwv_ref, nw_ref, o_ref, state_ref, *, n_heads):
    hd = pl.program_id(1)
    c = pl.program_id(2)
    rows = q_ref.shape[1]
    dk = q_ref.shape[2]
    ch = DN_CHUNK

    @pl.when(c == 0)
    def _():
        state_ref[...] = jnp.zeros_like(state_ref)

    def conv_silu(h_ref, x_ref, w_ref):
        halo = jnp.where(c > 0, h_ref[0], 0.0)
        return _silu(_causal_conv(halo, x_ref[0], w_ref[...]))

    q = _l2_normalize(conv_silu(qh_ref, q_ref, cwq_ref)) * (dk ** -0.5)
    k = _l2_normalize(conv_silu(kh_ref, k_ref, cwk_ref))
    v = conv_silu(vh_ref, v_ref, cwv_ref)

    a_scale = -jnp.exp(jnp.full((1, 1), alog_ref[hd], F32))
    dtb = dtb_ref[hd]
    sm = sm_ref[0]
    lane = lax.broadcasted_iota(jnp.int32, sm.shape, 1)
    a_col = jnp.sum(jnp.where(lane == hd, sm, 0.0), axis=1, keepdims=True)
    b_col = jnp.sum(jnp.where(lane == hd + n_heads, sm, 0.0), axis=1, keepdims=True)
    beta = _sigmoid(b_col)
    gc_col = _segment_cumsum(a_scale * _softplus(a_col + dtb), 0, ch)
    a_row = smt_ref[pl.ds(hd, 1), :]
    gc_row = _segment_cumsum(a_scale * _softplus(a_row + dtb), 1, ch)

    ri = lax.broadcasted_iota(jnp.int32, (ch, ch), 0)
    ci = lax.broadcasted_iota(jnp.int32, (ch, ch), 1)
    incl = ri >= ci
    strict = ri > ci
    eye = (ri == ci).astype(F32)

    state = state_ref[...]
    outs = []
    for n in range(rows // ch):
        sl = slice(n * ch, (n + 1) * ch)
        qn, kn, vn = q[sl], k[sl], v[sl]
        gcc = gc_col[sl]
        gcr = gc_row[:, sl]
        bn = beta[sl]
        decay = jnp.where(incl, jnp.exp(jnp.where(incl, gcc - gcr, 0.0)), 0.0)
        kb = kn * bn
        kb16, k16 = kb.astype(BF16), kn.astype(BF16)
        lower = jnp.where(strict, _dot_nt(kb16, k16) * decay, 0.0)
        p = -lower
        t_inv = eye + p
        m = 2
        while m < ch:
            p = _dot_hi(p, p)
            t_inv = t_inv + _dot_hi(t_inv, p)
            m *= 2
        t16 = t_inv.astype(BF16)
        eg = jnp.exp(gcc)
        u = _dot(t16, (vn * bn).astype(BF16))
        w = _dot(t16, (kb * eg).astype(BF16))
        attn = _dot_nt(qn.astype(BF16), k16) * decay
        qg = qn * eg
        g_end = gcc[ch - 1:ch]
        kg = kn * jnp.exp(g_end - gcc)
        s16 = state.astype(BF16)
        v_new = u - _dot(w.astype(BF16), s16)
        vn16 = v_new.astype(BF16)
        outs.append(_dot(qg.astype(BF16), s16) + _dot(attn.astype(BF16), vn16))
        state = state * jnp.exp(g_end) + _dot_tn(kg.astype(BF16), vn16)
    state_ref[...] = state
    o = jnp.concatenate(outs, axis=0) if len(outs) > 1 else outs[0]
    o = o * lax.rsqrt(jnp.mean(o * o, axis=-1, keepdims=True) + 1e-6) * nw_ref[...]
    o_ref[0] = (o * _silu(z_ref[0])).astype(o_ref.dtype)


def _deltanet(h3, small_t, conv_w, a_log, dt_bias, norm_w, *, n_heads, col0, small_col, rows=256):
    bsz, t_len, _ = h3.shape
    rows = min(rows, t_len)
    dk = DN_HEAD_DIM
    hb = rows // V7X_SUBLANES
    nblk = t_len // rows
    cb = col0 // dk
    smb = small_col // 256

    def halo(off):
        return pl.BlockSpec((1, V7X_SUBLANES, dk),
                            lambda b, h, c: (b, jnp.maximum(c * hb - 1, 0), cb + off * n_heads + h))

    def cur(off):
        return pl.BlockSpec((1, rows, dk), lambda b, h, c: (b, c, cb + off * n_heads + h))

    def convw(off):
        return pl.BlockSpec((DN_CONV, dk), lambda b, h, c: (0, off * n_heads + h))

    smem = pl.BlockSpec(memory_space=pltpu.SMEM)
    return pl.pallas_call(
        functools.partial(_dn_kernel, n_heads=n_heads),
        grid=(bsz, n_heads, nblk),
        in_specs=[
            smem, smem,
            halo(0), cur(0), halo(1), cur(1), halo(2), cur(2), cur(3),
            pl.BlockSpec((1, rows, 256), lambda b, h, c: (b, c, smb)),
            pl.BlockSpec((small_t.shape[0], rows), lambda b, h, c: (0, b * nblk + c)),
            convw(0), convw(1), convw(2),
            pl.BlockSpec((1, dk), lambda b, h, c: (0, 0)),
        ],
        out_specs=pl.BlockSpec((1, rows, dk), lambda b, h, c: (b, c, h)),
        out_shape=jax.ShapeDtypeStruct((bsz, t_len, n_heads * dk), BF16),
        scratch_shapes=[pltpu.VMEM((dk, dk), F32)],
        compiler_params=_params(("arbitrary", "arbitrary", "arbitrary")),
        name="deltanet",
    )(a_log, dt_bias, h3, h3, h3, h3, h3, h3, h3, h3, small_t,
      conv_w, conv_w, conv_w, norm_w.reshape(1, dk))


NSA_QB = 128
NSA_TK = 512
NSA_BIAS_TILES = -(-(REL_MAX_DIST + NSA_QB - 1) // NSA_QB) + 1


def _rel_bucket(dist):
    n = jnp.maximum(dist, 0)
    exact = REL_BUCKETS // 2
    big = exact + (jnp.log(jnp.maximum(n, 1).astype(jnp.float32) / exact)
                   / math.log(REL_MAX_DIST / exact) * (REL_BUCKETS - exact)).astype(jnp.int32)
    return jnp.where(n < exact, n, jnp.minimum(big, REL_BUCKETS - 1))


def _bias_kernel(rb_ref, bk_ref, o_ref):
    bk = bk_ref[...]
    for h in range(o_ref.shape[0]):
        acc = jnp.zeros(bk.shape, F32)
        for b in range(REL_BUCKETS):
            acc = jnp.where(bk == b, rb_ref[b, h], acc)
        o_ref[h] = acc


def _bias_table(rel_bias, buckets, rows=256):
    m, w = buckets.shape
    n_heads = rel_bias.shape[1]
    rows = min(rows, m)
    return pl.pallas_call(
        _bias_kernel,
        grid=(m // rows,),
        in_specs=[pl.BlockSpec(memory_space=pltpu.SMEM),
                  pl.BlockSpec((rows, w), lambda i: (i, 0))],
        out_specs=pl.BlockSpec((n_heads, rows, w), lambda i: (0, i, 0)),
        out_shape=jax.ShapeDtypeStruct((n_heads, m, w), F32),
        compiler_params=_params(("arbitrary",)),
        name="nsa_bias_table",
    )(rel_bias, buckets)


def _cmp_kernel(z_ref, pe_ref, w1_ref, w2_ref, o_ref):
    z = z_ref[0, 0]
    r, half = z.shape
    w1 = w1_ref[0]
    top = _dot(z, w1[:half])
    bot = _dot(z, w1[half:])
    bias = _dot(pe_ref[0], w1)[0:1]
    pre = top + pltpu.roll(bot, r - 1, 0) + bias
    o_ref[0, 0] = _dot(_gelu_tanh(pre).astype(BF16), w2_ref[0]).astype(o_ref.dtype)


def _nsa_compress(z, pe, w1, w2):
    bsz, _, r, half = z.shape
    d = w2.shape[-1]
    return pl.pallas_call(
        _cmp_kernel,
        grid=(bsz, 4),
        in_specs=[
            pl.BlockSpec((1, 1, r, half), lambda b, j: (b, j, 0, 0)),
            pl.BlockSpec((1, V7X_SUBLANES, 2 * half), lambda b, j: (j // 2, 0, 0)),
            pl.BlockSpec((1, 2 * half, w1.shape[-1]), lambda b, j: (j // 2, 0, 0)),
            pl.BlockSpec((1, w2.shape[1], d), lambda b, j: (j // 2, 0, 0)),
        ],
        out_specs=pl.BlockSpec((1, 1, r, d), lambda b, j: (b, j, 0, 0)),
        out_shape=jax.ShapeDtypeStruct((bsz, 4, r, d), BF16),
        compiler_params=_params(("arbitrary", "arbitrary")),
        name="nsa_compress",
    )(z, pe, w1, w2)


def _masked_softmax(logits, mask):
    logits = jnp.where(mask, logits, NEG)
    m = jnp.max(logits, axis=-1, keepdims=True)
    e = jnp.where(mask, jnp.exp(logits - m), 0.0)
    return e / jnp.maximum(jnp.sum(e, axis=-1, keepdims=True), 1e-30)


def _nsa_kernel(q_ref, sm_ref, kc_ref, vc_ref, bc_ref, bt_ref, ovt_ref,
                ks_ref, vs_ref, kw_ref, vw_ref, o_ref, *, n_cmp, n_top, gate_col):
    g = pl.program_id(1)
    qb = pl.program_id(2)
    qr = NSA_QB
    d = NSA_HEAD_DIM
    hpg = q_ref.shape[2] // d
    s0 = qb * qr
    q = q_ref[0] * (d ** -0.5)
    q4 = jnp.concatenate([q[:, h * d:(h + 1) * d] for h in range(hpg)], axis=0).astype(BF16)
    t_col = s0 + lax.broadcasted_iota(jnp.int32, (qr, 1), 0)
    t4 = jnp.concatenate([t_col] * hpg, axis=0)

    r = kc_ref.shape[2]
    n_sel = ovt_ref.shape[0]
    sc = _dot_nt(q4, kc_ref[0, 0]) + bc_ref[...].reshape(hpg * qr, r)
    n_row = lax.broadcasted_iota(jnp.int32, (1, r), 1)
    cmp_end = n_row * NSA_CMP_STRIDE + (NSA_CMP_BLOCK - 1)
    p4 = _masked_softmax(sc, (cmp_end <= t4) & (n_row < n_cmp))
    o_cmp = _dot(p4.astype(BF16), vc_ref[0, 0])

    p_sum = p4[0:qr]
    for h in range(1, hpg):
        p_sum = p_sum + p4[h * qr:(h + 1) * qr]
    p_hi = p_sum.astype(BF16)
    p_lo = (p_sum - p_hi.astype(F32)).astype(BF16)
    ovt = ovt_ref[...]
    imp = _dot_nt(ovt, p_hi) + _dot_nt(ovt, p_lo)
    j_col = lax.broadcasted_iota(jnp.int32, (n_sel, 1), 0)
    t_row = s0 + lax.broadcasted_iota(jnp.int32, (1, qr), 1)
    q_blk = t_row // NSA_SEL_BLOCK
    forced = (j_col == 0) | (j_col == q_blk) | (j_col == q_blk - 1)
    score = jnp.where(j_col * NSA_SEL_BLOCK <= t_row,
                      jnp.where(forced, FORCE_SCORE, imp), -FORCE_SCORE)
    rank = jnp.zeros((n_sel, qr), jnp.int32)
    for k in range(n_sel):
        row = score[k:k + 1]
        ahead = (row > score) | ((row == score) & (j_col > k))
        rank = rank + ahead.astype(jnp.int32)
    sel_t = (rank < n_top).astype(BF16)

    tk = NSA_TK
    key_lane = lax.broadcasted_iota(jnp.int32, (1, tk), 1)
    nb = bt_ref.shape[0]

    def sel_step(j, carry):
        m_run, l_run, acc = carry
        k0 = pl.multiple_of(j * tk, tk)
        kt = ks_ref[0, 0, pl.ds(k0, tk), :]
        vt = vs_ref[0, 0, pl.ds(k0, tk), :]
        bias = jnp.concatenate(
            [bt_ref[jnp.clip(qb - (tk // qr) * j - mm, 0, nb - 1)].reshape(hpg * qr, qr)
             for mm in range(tk // qr)], axis=1)
        s = _dot_nt(q4, kt) + bias
        key = k0 + key_lane
        expand = (j_col == key // NSA_SEL_BLOCK).astype(BF16)
        chosen = _dot_tn(sel_t, expand)
        mask = (chosen > 0.5) & (key <= t_col)
        mask4 = jnp.concatenate([mask] * hpg, axis=0)
        s = jnp.where(mask4, s, NEG)
        m_new = jnp.maximum(m_run, jnp.max(s, axis=-1, keepdims=True))
        scale = jnp.exp(m_run - m_new)
        e = jnp.where(mask4, jnp.exp(s - m_new), 0.0)
        l_new = l_run * scale + jnp.sum(e, axis=-1, keepdims=True)
        acc = acc * scale + _dot(e.astype(BF16), vt)
        return m_new, l_new, acc

    init = (jnp.full((hpg * qr, 1), NEG, F32), jnp.zeros((hpg * qr, 1), F32),
            jnp.zeros((hpg * qr, d), F32))
    _, l_sel, acc_sel = lax.fori_loop(0, (s0 + qr - 1) // tk + 1, sel_step, init)
    o_sel = acc_sel / jnp.maximum(l_sel, 1e-30)

    n_wt = NSA_WINDOW // qr + 1
    kws, vws, biases, masks = [], [], [], []
    i_col = lax.broadcasted_iota(jnp.int32, (qr, 1), 0)
    j_row = lax.broadcasted_iota(jnp.int32, (1, qr), 1)
    for mm in range(n_wt):
        k0 = s0 - NSA_WINDOW + mm * qr
        k0c = pl.multiple_of(jnp.maximum(k0, 0), qr)
        kws.append(kw_ref[0, 0, pl.ds(k0c, qr), :])
        vws.append(vw_ref[0, 0, pl.ds(k0c, qr), :])
        biases.append(bt_ref[n_wt - 1 - mm].reshape(hpg * qr, qr))
        dist = (n_wt - 1 - mm) * qr + i_col - j_row
        masks.append((dist >= 0) & (dist < NSA_WINDOW) & (k0 >= 0))
    sw = _dot_nt(q4, jnp.concatenate(kws, axis=0)) + jnp.concatenate(biases, axis=1)
    mw = jnp.concatenate(masks, axis=1)
    pw = _masked_softmax(sw, jnp.concatenate([mw] * hpg, axis=0))
    o_win = _dot(pw.astype(BF16), jnp.concatenate(vws, axis=0))

    sm = sm_ref[0]
    lane = lax.broadcasted_iota(jnp.int32, sm.shape, 1)

    def gate(h, branch):
        col = gate_col + 3 * (g * hpg + h) + branch
        return _sigmoid(jnp.sum(jnp.where(lane == col, sm, 0.0), axis=1, keepdims=True))

    outs = []
    for h in range(hpg):
        sl = slice(h * qr, (h + 1) * qr)
        outs.append(gate(h, 0) * o_cmp[sl] + gate(h, 1) * o_sel[sl] + gate(h, 2) * o_win[sl])
    o_ref[0] = jnp.concatenate(outs, axis=1).astype(o_ref.dtype)


def _nsa_attention(h3, kv_t, cmp_kv, bias_c, bias_t, ovt, *, q_col, small_col, gate_col, n_cmp):
    bsz, t_len, _ = h3.shape
    g_n = NSA_KV_GROUPS
    d = NSA_HEAD_DIM
    n_heads = bias_c.shape[0]
    hpg = n_heads // g_n
    qw = hpg * d
    r = cmp_kv.shape[2]
    n_sel = t_len // NSA_SEL_BLOCK
    nb = bias_t.shape[0]
    kv_spec = lambda part: pl.BlockSpec((1, 1, t_len, d), lambda b, g, q: (b, 2 * part + g, 0, 0))
    return pl.pallas_call(
        functools.partial(_nsa_kernel, n_cmp=n_cmp, n_top=min(NSA_TOP_BLOCKS, n_sel),
                          gate_col=gate_col),
        grid=(bsz, g_n, t_len // NSA_QB),
        in_specs=[
            pl.BlockSpec((1, NSA_QB, qw), lambda b, g, q: (b, q, q_col // qw + g)),
            pl.BlockSpec((1, NSA_QB, 256), lambda b, g, q: (b, q, small_col // 256)),
            pl.BlockSpec((1, 1, r, d), lambda b, g, q: (b, g, 0, 0)),
            pl.BlockSpec((1, 1, r, d), lambda b, g, q: (b, g_n + g, 0, 0)),
            pl.BlockSpec((hpg, NSA_QB, r), lambda b, g, q: (g, q, 0)),
            pl.BlockSpec((nb, hpg, NSA_QB, NSA_QB), lambda b, g, q: (0, g, 0, 0)),
            pl.BlockSpec((n_sel, r), lambda b, g, q: (0, 0)),
            kv_spec(2), kv_spec(3), kv_spec(4), kv_spec(5),
        ],
        out_specs=pl.BlockSpec((1, NSA_QB, qw), lambda b, g, q: (b, q, g)),
        out_shape=jax.ShapeDtypeStruct((bsz, t_len, n_heads * d), BF16),
        compiler_params=_params(("arbitrary", "arbitrary", "arbitrary")),
        name="nsa_attention",
    )(h3, h3, cmp_kv, cmp_kv, bias_c, bias_t, ovt, kv_t, kv_t, kv_t, kv_t)


def _outproj_kernel(x_ref, ya_ref, yb_ref, yc_ref, wa_ref, wb_ref, wc_ref, g_ref, b_ref, o_ref,
                    *, alpha):
    mix = _dot(ya_ref[...], wa_ref[...]) + _dot(yb_ref[...], wb_ref[...]) + _dot(yc_ref[...], wc_ref[...])
    o_ref[...] = _layer_norm(alpha * x_ref[...] + mix, g_ref[...], b_ref[...])


def _outproj_ln(x2d, ya, yb, yc, w_out, g, b, alpha, tm=256):
    n, d = x2d.shape
    wa_n, wb_n, wc_n = ya.shape[1], yb.shape[1], yc.shape[1]
    w16 = w_out.astype(BF16)
    row = lambda w: pl.BlockSpec((tm, w), lambda i: (i, 0))
    full = lambda r: pl.BlockSpec((r, d), lambda i: (0, 0))
    return pl.pallas_call(
        functools.partial(_outproj_kernel, alpha=alpha),
        grid=(n // tm,),
        in_specs=[row(d), row(wa_n), row(wb_n), row(wc_n),
                  full(wa_n), full(wb_n), full(wc_n), full(1), full(1)],
        out_specs=row(d),
        out_shape=jax.ShapeDtypeStruct((n, d), F32),
        compiler_params=_params(("arbitrary",)),
        name="outproj_ln",
    )(x2d, ya, yb, yc, w16[:wa_n], w16[wa_n:wa_n + wb_n], w16[wa_n + wb_n:],
      g.reshape(1, d), b.reshape(1, d))


def _mm_kernel(x_ref, w_ref, o_ref):
    o_ref[...] = _dot(x_ref[...].astype(BF16), w_ref[...]).astype(o_ref.dtype)


def _matmul(x2d, w16, out_dtype, tm=256):
    n, d = x2d.shape
    cols = w16.shape[1]
    tm = min(tm, n)
    return pl.pallas_call(
        _mm_kernel,
        grid=(n // tm,),
        in_specs=[pl.BlockSpec((tm, d), lambda i: (i, 0)),
                  pl.BlockSpec((d, cols), lambda i: (0, 0))],
        out_specs=pl.BlockSpec((tm, cols), lambda i: (i, 0)),
        out_shape=jax.ShapeDtypeStruct((n, cols), out_dtype),
        compiler_params=_params(("arbitrary",)),
        name="mem_kv_proj",
    )(x2d, w16)


def _xattn_kernel(x_ref, kv_ref, wq_ref, wo_ref, g_ref, b_ref, o_ref, o16_ref, *, alpha):
    x = x_ref[...]
    q = _dot(x.astype(BF16), wq_ref[...])
    kv = kv_ref[0]
    width = wq_ref.shape[1]
    hd = XA_HEAD_DIM
    outs = []
    for h in range(width // hd):
        qh = q[:, h * hd:(h + 1) * hd].astype(BF16)
        kh = kv[:, h * hd:(h + 1) * hd]
        vh = kv[:, width + h * hd:width + (h + 1) * hd]
        logits = _dot_nt(qh, kh) * (hd ** -0.5)
        m = jnp.max(logits, axis=-1, keepdims=True)
        e = jnp.exp(logits - m)
        p = e / jnp.sum(e, axis=-1, keepdims=True)
        outs.append(_dot(p.astype(BF16), vh))
    o = jnp.concatenate(outs, axis=1).astype(BF16)
    y = _layer_norm(alpha * x + _dot(o, wo_ref[...]), g_ref[...], b_ref[...])
    o_ref[...] = y
    o16_ref[...] = y.astype(BF16)


def _cross_attention_ln(x2d, kv, wq, wo, g, b, alpha, t_len, tm=256):
    n, d = x2d.shape
    width = wq.shape[1]
    m_len = kv.shape[1]
    per_b = t_len // tm
    return pl.pallas_call(
        functools.partial(_xattn_kernel, alpha=alpha),
        grid=(n // tm,),
        in_specs=[
            pl.BlockSpec((tm, d), lambda i: (i, 0)),
            pl.BlockSpec((1, m_len, 2 * width), lambda i: (i // per_b, 0, 0)),
            pl.BlockSpec((d, width), lambda i: (0, 0)),
            pl.BlockSpec((width, d), lambda i: (0, 0)),
            pl.BlockSpec((1, d), lambda i: (0, 0)),
            pl.BlockSpec((1, d), lambda i: (0, 0)),
        ],
        out_specs=[pl.BlockSpec((tm, d), lambda i: (i, 0)), pl.BlockSpec((tm, d), lambda i: (i, 0))],
        out_shape=[jax.ShapeDtypeStruct((n, d), F32), jax.ShapeDtypeStruct((n, d), BF16)],
        compiler_params=_params(("arbitrary",)),
        name="cross_attention_ln",
    )(x2d, kv, wq.astype(BF16), wo.astype(BF16), g.reshape(1, d), b.reshape(1, d))


def _rank_rows(v, idx_col):
    rank = jnp.zeros(v.shape, jnp.int32)
    for k in range(v.shape[0]):
        row = v[k:k + 1]
        rank = rank + ((row > v) | ((row == v) & (idx_col > k))).astype(jnp.int32)
    return rank


def _router_kernel(x_ref, w_ref, b_ref, e_ref, g_ref):
    n_exp = w_ref.shape[0]
    gsz = n_exp // N_GROUPS
    logits = lax.dot_general(w_ref[...], x_ref[...], (((1,), (1,)), ((), ())),
                             preferred_element_type=F32, precision=lax.Precision.HIGHEST)
    scores = _sigmoid(logits)
    choice = scores + b_ref[...]
    tm = scores.shape[1]
    sub = lax.broadcasted_iota(jnp.int32, (gsz, 1), 0)
    grp_rows = []
    for gi in range(N_GROUPS):
        slab = choice[gi * gsz:(gi + 1) * gsz]
        m1 = jnp.max(slab, axis=0, keepdims=True)
        first = jnp.min(jnp.where(slab == m1, sub, gsz), axis=0, keepdims=True)
        m2 = jnp.max(jnp.where(sub == first, NEG, slab), axis=0, keepdims=True)
        grp_rows.append(m1 + m2)
    grp = jnp.concatenate(grp_rows, axis=0)
    g_idx = lax.broadcasted_iota(jnp.int32, (N_GROUPS, 1), 0)
    grp_ok = _rank_rows(grp, g_idx) < TOPK_GROUPS
    ok = jnp.concatenate([jnp.broadcast_to(grp_ok[gi:gi + 1], (gsz, tm)) for gi in range(N_GROUPS)], axis=0)
    e_idx = lax.broadcasted_iota(jnp.int32, (n_exp, 1), 0)
    rank = _rank_rows(jnp.where(ok, choice, -1e9), e_idx)
    denom = jnp.sum(jnp.where(rank < TOP_K, scores, 0.0), axis=0, keepdims=True)
    e_rows, g_rows = [], []
    for k in range(TOP_K):
        hit = rank == k
        e_rows.append(jnp.sum(jnp.where(hit, e_idx, 0), axis=0, keepdims=True))
        g_rows.append(jnp.sum(jnp.where(hit, scores, 0.0), axis=0, keepdims=True) / denom * ROUTED_SCALE)
    e_ref[...] = jnp.concatenate(e_rows, axis=0)
    g_ref[...] = jnp.concatenate(g_rows, axis=0)


def _router(x2d, w_router, bias, tm=512):
    n, d = x2d.shape
    n_exp = w_router.shape[1]
    return pl.pallas_call(
        _router_kernel,
        grid=(n // tm,),
        in_specs=[pl.BlockSpec((tm, d), lambda i: (i, 0)),
                  pl.BlockSpec((n_exp, d), lambda i: (0, 0)),
                  pl.BlockSpec((n_exp, 1), lambda i: (0, 0))],
        out_specs=[pl.BlockSpec((TOP_K, tm), lambda i: (0, i)), pl.BlockSpec((TOP_K, tm), lambda i: (0, i))],
        out_shape=[jax.ShapeDtypeStruct((TOP_K, n), jnp.int32), jax.ShapeDtypeStruct((TOP_K, n), F32)],
        compiler_params=_params(("arbitrary",)),
        name="moe_router",
    )(x2d, w_router.T, bias.reshape(n_exp, 1))


def _expert_kernel(be_ref, first_ref, nv_ref,
                   tok_ref, dst_ref, x_hbm, wt_ref, wg_ref, wu_ref, wd_ref, y_hbm,
                   xbuf, obuf, wg16, wu16, wd16, gsem, ssem):
    i = pl.program_id(0)
    nv = nv_ref[i]
    rows = wt_ref.shape[1]
    cpr = xbuf.shape[0] // rows

    def slab(ref, r):
        return ref.at[pl.ds(pl.multiple_of(r * cpr, cpr), cpr)]

    @pl.when(i == 0)
    def _():
        xbuf[...] = jnp.zeros_like(xbuf)

    @pl.when(nv > 0)
    def _():
        def gather(r, _):
            pltpu.make_async_copy(slab(x_hbm, tok_ref[0, 0, r]), slab(xbuf, r), gsem).start()
            return 0

        lax.fori_loop(0, nv, gather, 0)

        @pl.when(first_ref[i] == 1)
        def _():
            wg16[...] = wg_ref[0].astype(BF16)
            wu16[...] = wu_ref[0].astype(BF16)
            wd16[...] = wd_ref[0].astype(BF16)

        def gather_wait(r, _):
            pltpu.make_async_copy(slab(x_hbm, 0), slab(xbuf, r), gsem).wait()
            return 0

        lax.fori_loop(0, nv, gather_wait, 0)
        xb = jnp.concatenate([xbuf[pl.ds(c, rows, stride=cpr), :] for c in range(cpr)], axis=1)
        xb = xb.astype(BF16)
        hid = _silu(_dot(xb, wg16[...])) * _dot(xb, wu16[...])
        out = _dot(hid.astype(BF16), wd16[...]) * wt_ref[0]
        for c in range(cpr):
            obuf[pl.ds(c, rows, stride=cpr), :] = out[:, c * V7X_LANES:(c + 1) * V7X_LANES]

        def scatter(r, _):
            pltpu.make_async_copy(slab(obuf, r), slab(y_hbm, dst_ref[0, 0, r]), ssem).start()
            return 0

        lax.fori_loop(0, nv, scatter, 0)

        def scatter_wait(r, _):
            pltpu.make_async_copy(slab(obuf, r), slab(y_hbm, 0), ssem).wait()
            return 0

        lax.fori_loop(0, nv, scatter_wait, 0)


def _experts(x2d, block_e, first, n_valid, row_tok, row_dst, row_w, w_gate, w_up, w_down):
    n, d = x2d.shape
    n_blocks = block_e.shape[0]
    dff = w_gate.shape[2]
    rows = MOE_ROWS
    cpr = d // V7X_LANES
    wspec = lambda shape: pl.BlockSpec((1,) + shape, lambda i, be, *_: (be[i], 0, 0))
    ispec = pl.BlockSpec((1, 1, rows), lambda i, *_: (i, 0, 0), memory_space=pltpu.SMEM)
    grid_spec = pltpu.PrefetchScalarGridSpec(
        num_scalar_prefetch=3,
        grid=(n_blocks,),
        in_specs=[
            ispec, ispec,
            pl.BlockSpec(memory_space=pl.ANY),
            pl.BlockSpec((1, rows, 1), lambda i, *_: (i, 0, 0)),
            wspec((d, dff)), wspec((d, dff)), wspec((dff, d)),
        ],
        out_specs=pl.BlockSpec(memory_space=pl.ANY),
        scratch_shapes=[
            pltpu.VMEM((rows * cpr, V7X_LANES), F32), pltpu.VMEM((rows * cpr, V7X_LANES), F32),
            pltpu.VMEM((d, dff), BF16), pltpu.VMEM((d, dff), BF16), pltpu.VMEM((dff, d), BF16),
            pltpu.SemaphoreType.DMA, pltpu.SemaphoreType.DMA,
        ],
    )
    return pl.pallas_call(
        _expert_kernel,
        grid_spec=grid_spec,
        out_shape=jax.ShapeDtypeStruct((n * TOP_K * cpr, V7X_LANES), F32),
        compiler_params=_params(("arbitrary",)),
        name="moe_experts",
    )(block_e, first, n_valid,
      row_tok.reshape(n_blocks, 1, rows), row_dst.reshape(n_blocks, 1, rows),
      x2d.reshape(n * cpr, V7X_LANES), row_w.reshape(n_blocks, rows, 1), w_gate, w_up, w_down)


def _combine_kernel(x_ref, x16_ref, y_ref, wg_ref, wu_ref, wd_ref, g_ref, b_ref, o_ref, acc_ref,
                    *, alpha):
    x16 = x16_ref[...]
    shared = _dot((_silu(_dot(x16, wg_ref[...])) * _dot(x16, wu_ref[...])).astype(BF16), wd_ref[...])
    tm, d = x_ref.shape
    cpr = d // V7X_LANES
    y = y_ref[...].reshape(tm, TOP_K, cpr, V7X_LANES)
    tot = y[:, 0]
    for k in range(1, TOP_K):
        tot = tot + y[:, k]
    acc_ref[...] = tot.reshape(tm * cpr, V7X_LANES)
    routed = jnp.concatenate([acc_ref[pl.ds(c, tm, stride=cpr), :] for c in range(cpr)], axis=1)
    o_ref[...] = _layer_norm(alpha * x_ref[...] + (routed + shared), g_ref[...], b_ref[...])


def _moe_combine_ln(x2d, x16, y8, ws_gate, ws_up, ws_down, g, b, alpha, tm=128):
    n, d = x2d.shape
    dff = ws_gate.shape[1]
    cpr = d // V7X_LANES
    row = lambda w: pl.BlockSpec((tm, w), lambda i: (i, 0))
    full = lambda r, c: pl.BlockSpec((r, c), lambda i: (0, 0))
    return pl.pallas_call(
        functools.partial(_combine_kernel, alpha=alpha),
        grid=(n // tm,),
        in_specs=[row(d), row(d), pl.BlockSpec((tm * TOP_K * cpr, V7X_LANES), lambda i: (i, 0)),
                  full(d, dff), full(d, dff), full(dff, d), full(1, d), full(1, d)],
        out_specs=row(d),
        out_shape=jax.ShapeDtypeStruct((n, d), F32),
        scratch_shapes=[pltpu.VMEM((tm * cpr, V7X_LANES), F32)],
        compiler_params=_params(("arbitrary",)),
        name="moe_combine_ln",
    )(x2d, x16, y8, ws_gate.astype(BF16), ws_up.astype(BF16),
      ws_down.astype(BF16), g.reshape(1, d), b.reshape(1, d))


def _dispatch_plan(top_e, gate, n_exp):
    n = top_e.shape[1]
    rows = MOE_ROWS
    n_assign = n * TOP_K
    n_blocks = n_assign // rows + n_exp
    flat_e = top_e.T.reshape(-1)
    flat_w = gate.T.reshape(-1)
    order = jnp.argsort(flat_e, stable=True).astype(jnp.int32)
    e_sorted = flat_e[order]
    experts = jnp.arange(n_exp, dtype=jnp.int32)
    starts = jnp.searchsorted(e_sorted, experts, side='left').astype(jnp.int32)
    counts = jnp.searchsorted(e_sorted, experts, side='right').astype(jnp.int32) - starts
    padded = (counts + rows - 1) // rows * rows
    pad_ends = jnp.cumsum(padded)
    pad_starts = pad_ends - padded
    blk_start = jnp.arange(n_blocks, dtype=jnp.int32) * rows
    block_e = jnp.minimum(jnp.searchsorted(pad_ends, blk_start, side='right'), n_exp - 1).astype(jnp.int32)
    used = blk_start < pad_ends[-1]
    j0 = blk_start - pad_starts[block_e]
    n_valid = jnp.where(used, jnp.clip(counts[block_e] - j0, 0, rows), 0).astype(jnp.int32)
    first = (used & (j0 == 0)).astype(jnp.int32)
    r = jnp.arange(n_blocks * rows, dtype=jnp.int32)
    e_r = block_e[r // rows]
    j_r = r - pad_starts[e_r]
    valid = (j_r < counts[e_r]) & used[r // rows]
    a_r = order[jnp.clip(starts[e_r] + j_r, 0, n_assign - 1)]
    row_tok = jnp.where(valid, a_r // TOP_K, 0).astype(jnp.int32)
    row_dst = jnp.where(valid, a_r, 0).astype(jnp.int32)
    row_w = jnp.where(valid, flat_w[a_r], 0.0)
    return block_e, first, n_valid, row_tok, row_dst, row_w


def kernel(x, mem, rel_bias, w_in, rg_conv_w, rg_conv_b, rg_wa, rg_ba, rg_wi, rg_bi, rg_lambda,
           dn_conv_w, dn_a_log, dn_dt_bias, dn_norm_w, nsa_pe_k, nsa_pe_v, nsa_phi_k1, nsa_phi_k2,
           nsa_phi_v1, nsa_phi_v2, w_out, ln1_g, ln1_b, xa_wq, xa_wk, xa_wv, xa_wo, ln2_g, ln2_b,
           moe_router, moe_router_bias, moe_w_gate, moe_w_up, moe_w_down, shared_w_gate, shared_w_up,
           shared_w_down, ln3_g, ln3_b):
    bsz, t_len, d_model = x.shape
    depth = w_in.shape[0]
    n_tok = bsz * t_len
    alpha = (2 * depth) ** 0.25
    rg_w = rg_conv_w.shape[2]
    dn_w = dn_conv_w.shape[2] // 3
    dn_heads = dn_a_log.shape[1]
    nsa_heads = rel_bias.shape[1]
    nsa_w = nsa_heads * NSA_HEAD_DIM
    kv_w = NSA_KV_GROUPS * NSA_HEAD_DIM
    n_exp = moe_router.shape[2]

    sizes = (rg_w, rg_w, dn_w, dn_w, dn_w, dn_w, dn_heads, dn_heads, nsa_w, 6 * kv_w, 3 * nsa_heads)
    offs = [0]
    for s in sizes:
        offs.append(offs[-1] + s)
    col = lambda w, i: w[:, :, offs[i]:offs[i + 1]]
    main = offs[6]
    q_col = main
    small_col = q_col + nsa_w
    kv_col = small_col + 256
    gate_col = 2 * dn_heads
    small = jnp.concatenate([col(w_in, 6), col(w_in, 7), col(w_in, 10)], axis=2)
    small = jnp.pad(small, ((0, 0), (0, 0), (0, 256 - small.shape[2])))
    w_slab = jnp.concatenate([w_in[:, :, :main], col(w_in, 8), small, col(w_in, 9)], axis=2).astype(BF16)
    w_small_t = jnp.swapaxes(jnp.concatenate([col(w_in, 6), col(w_in, 7)], axis=2), 1, 2).astype(BF16)

    r_cmp = t_len // NSA_CMP_STRIDE
    n_cmp = (t_len - NSA_CMP_BLOCK) // NSA_CMP_STRIDE + 1
    n_sel = t_len // NSA_SEL_BLOCK
    cmp_start = jnp.arange(r_cmp) * NSA_CMP_STRIDE
    cmp_end = cmp_start + NSA_CMP_BLOCK - 1
    sel_start = jnp.arange(n_sel) * NSA_SEL_BLOCK
    ovt = ((cmp_start[None, :] < sel_start[:, None] + NSA_SEL_BLOCK)
           & (cmp_end[None, :] >= sel_start[:, None])
           & (jnp.arange(r_cmp)[None, :] < n_cmp)).astype(BF16)
    bucket_c = _rel_bucket(jnp.arange(t_len)[:, None] - cmp_end[None, :]).astype(jnp.int32)
    tile = jnp.arange(NSA_BIAS_TILES * NSA_QB)[:, None] - jnp.arange(NSA_QB)[None, :]
    bucket_t = _rel_bucket(tile).astype(jnp.int32)
    bias_c = _bias_table(rel_bias, bucket_c)
    bias_t = _bias_table(rel_bias, bucket_t).reshape(nsa_heads, NSA_BIAS_TILES, NSA_QB, NSA_QB)
    bias_t = jnp.swapaxes(bias_t, 0, 1)

    cmp_in = NSA_CMP_BLOCK * NSA_HEAD_DIM
    x2d = x.reshape(n_tok, d_model)
    mem2d = mem.reshape(-1, d_model)
    for l in range(depth):
        h, small_t = _inproj(x2d, w_slab[l], w_small_t[l])
        h3 = h.reshape(bsz, t_len, -1)
        y_a = _rglru(h3, rg_conv_w[l], rg_conv_b[l], rg_wa[l], rg_wi[l], rg_ba[l], rg_bi[l],
                     rg_lambda[l], rg_w)
        y_b = _deltanet(h3, small_t, dn_conv_w[l], dn_a_log[l], dn_dt_bias[l], dn_norm_w[l],
                        n_heads=dn_heads, col0=2 * rg_w, small_col=small_col)
        kv_t = h3[:, :, kv_col:].reshape(bsz, t_len, 6 * NSA_KV_GROUPS, NSA_HEAD_DIM)
        kv_t = jnp.swapaxes(kv_t, 1, 2).astype(BF16)
        z = kv_t[:, :2 * NSA_KV_GROUPS].reshape(bsz, 2 * NSA_KV_GROUPS, r_cmp, cmp_in // 2)
        pe = jnp.stack([nsa_pe_k[l], nsa_pe_v[l]]).reshape(2, 1, cmp_in)
        pe = jnp.broadcast_to(pe, (2, V7X_SUBLANES, cmp_in)).astype(BF16)
        w1 = jnp.stack([nsa_phi_k1[l], nsa_phi_v1[l]]).astype(BF16)
        w2 = jnp.stack([nsa_phi_k2[l], nsa_phi_v2[l]]).astype(BF16)
        cmp_kv = _nsa_compress(z, pe, w1, w2)
        y_c = _nsa_attention(h3, kv_t, cmp_kv, bias_c, bias_t, ovt, q_col=q_col,
                             small_col=small_col, gate_col=gate_col, n_cmp=n_cmp)
        x2d = _outproj_ln(x2d, y_a.reshape(n_tok, -1), y_b.reshape(n_tok, -1), y_c.reshape(n_tok, -1),
                          w_out[l], ln1_g[l], ln1_b[l], alpha)
        w_kv = jnp.concatenate([xa_wk[l], xa_wv[l]], axis=1).astype(BF16)
        kv_mem = _matmul(mem2d, w_kv, BF16).reshape(bsz, mem.shape[1], -1)
        x2d, x16 = _cross_attention_ln(x2d, kv_mem, xa_wq[l], xa_wo[l], ln2_g[l], ln2_b[l], alpha, t_len)
        top_e, gate = _router(x2d, moe_router[l], moe_router_bias[l])
        plan = _dispatch_plan(top_e, gate, n_exp)
        y8 = _experts(x2d, *plan, moe_w_gate[l], moe_w_up[l], moe_w_down[l])
        x2d = _moe_combine_ln(x2d, x16, y8, shared_w_gate[l], shared_w_up[l], shared_w_down[l],
                              ln3_g[l], ln3_b[l], alpha)
    return x2d.reshape(bsz, t_len, d_model)
```

```python
import functools
import math

import jax
import jax.numpy as jnp
from jax import lax
from jax.experimental import pallas as pl
from jax.experimental.pallas import tpu as pltpu

F32 = jnp.float32
BF16 = jnp.bfloat16

LN_EPS = 1e-5
NEG = -1e30

RG_BLOCKS = 4
RG_CONV = 4
RG_C = 8.0
DN_HEAD_DIM = 128
DN_CONV = 4
DN_CHUNK = 64
NSA_HEAD_DIM = 64
NSA_KV_GROUPS = 2
NSA_CMP_BLOCK = 32
NSA_CMP_STRIDE = 16
NSA_SEL_BLOCK = 64
NSA_TOP_BLOCKS = 16
NSA_WINDOW = 512
FORCE_SCORE = 1e4
REL_BUCKETS = 32
REL_MAX_DIST = 1024
XA_HEADS = 4
XA_HEAD_DIM = 128
TOP_K = 8
N_GROUPS = 8
TOPK_GROUPS = 4
ROUTED_SCALE = 2.5
MOE_ROWS = 256

V7X_LANES = 128
V7X_SUBLANES = 8
V7X_VMEM_LIMIT = 56 * 1024 * 1024


def _params(semantics, vmem=V7X_VMEM_LIMIT):
    return pltpu.CompilerParams(dimension_semantics=semantics, vmem_limit_bytes=vmem)


def _sigmoid(x):
    return 1.0 / (1.0 + jnp.exp(-x))


def _softplus(x):
    return jnp.maximum(x, 0.0) + jnp.log1p(jnp.exp(-jnp.abs(x)))


def _silu(x):
    return x * _sigmoid(x)


def _gelu_tanh(x):
    c = math.sqrt(2.0 / math.pi)
    return x * (0.5 * (1.0 + jnp.tanh(c * (x + 0.044715 * (x * x * x)))))


def _layer_norm(v, g, b):
    mu = jnp.mean(v, axis=-1, keepdims=True)
    vc = v - mu
    var = jnp.mean(vc * vc, axis=-1, keepdims=True)
    return vc * lax.rsqrt(var + LN_EPS) * g + b


def _dot(a, b):
    return jnp.dot(a, b, preferred_element_type=F32)


def _dot_nt(a, b):
    return lax.dot_general(a, b, (((1,), (1,)), ((), ())), preferred_element_type=F32)


def _dot_tn(a, b):
    return lax.dot_general(a, b, (((0,), (0,)), ((), ())), preferred_element_type=F32)


def _inproj_kernel(x_ref, w_ref, wt_ref, o_ref, ot_ref, xb_ref):
    @pl.when(pl.program_id(1) == 0)
    def _():
        xb_ref[...] = x_ref[...].astype(BF16)
        ot_ref[...] = _dot_nt(wt_ref[...], xb_ref[...])

    o_ref[...] = _dot(xb_ref[...], w_ref[...])


def _inproj(x2d, w, wt, layer, tm=512, tn=512):
    n, d = x2d.shape
    cols = w.shape[2]
    return pl.pallas_call(
        _inproj_kernel,
        grid=(n // tm, cols // tn),
        in_specs=[
            pl.BlockSpec((tm, d), lambda i, j: (i, 0)),
            pl.BlockSpec((None, d, tn), lambda i, j: (layer, 0, j)),
            pl.BlockSpec((None, wt.shape[1], d), lambda i, j: (layer, 0, 0)),
        ],
        out_specs=[
            pl.BlockSpec((tm, tn), lambda i, j: (i, j)),
            pl.BlockSpec((wt.shape[1], tm), lambda i, j: (0, i)),
        ],
        out_shape=[
            jax.ShapeDtypeStruct((n, cols), F32),
            jax.ShapeDtypeStruct((wt.shape[1], n), F32),
        ],
        scratch_shapes=[pltpu.VMEM((tm, d), BF16)],
        compiler_params=_params(("arbitrary", "arbitrary")),
        name="inproj",
    )(x2d, w, wt)


def _shifted_rows(halo, x, s):
    cat = jnp.concatenate([halo, x], axis=0)
    return pltpu.roll(cat, s, 0)[V7X_SUBLANES:]


def _causal_conv(halo, x, w):
    k = w.shape[0]
    y = x * w[k - 1:k]
    for s in range(1, k):
        y = y + _shifted_rows(halo, x, s) * w[k - 1 - s:k - s]
    return y


def _rg_kernel(xh_ref, x_ref, g_ref, cw_ref, cb_ref, wa_ref, wi_ref, ba_ref, bi_ref, lam_ref,
               o_ref, h_ref):
    c = pl.program_id(1)
    x = x_ref[0]
    rows, width = x.shape
    blk = width // RG_BLOCKS
    halo = jnp.where(c > 0, xh_ref[0], 0.0)
    xc = _causal_conv(halo, x, cw_ref[...]) + cb_ref[...]
    xcb = xc.astype(BF16)
    r_parts, i_parts = [], []
    for n in range(RG_BLOCKS):
        xg = xcb[:, n * blk:(n + 1) * blk]
        r_parts.append(_dot(xg, wa_ref[n]))
        i_parts.append(_dot(xg, wi_ref[n]))
    r = _sigmoid(jnp.concatenate(r_parts, axis=1) + ba_ref[...])
    gi = _sigmoid(jnp.concatenate(i_parts, axis=1) + bi_ref[...])
    log_a = -RG_C * r * _softplus(-lam_ref[...])
    a = jnp.exp(log_a)
    th = jnp.tanh(log_a)
    b = jnp.sqrt(-2.0 * th / (1.0 - th)) * (gi * xc)
    row = lax.broadcasted_iota(jnp.int32, (rows, 1), 0)
    s = 1
    while s < rows:
        a_s = pltpu.roll(a, s, 0)
        b_s = pltpu.roll(b, s, 0)
        m = row >= s
        b = jnp.where(m, a * b_s + b, b)
        a = jnp.where(m, a * a_s, a)
        s *= 2

    @pl.when(c == 0)
    def _():
        h_ref[...] = jnp.zeros_like(h_ref)

    h = a * h_ref[0:1] + b
    h_ref[...] = jnp.broadcast_to(h[rows - 1:rows], h_ref.shape)
    o_ref[0] = (h * _gelu_tanh(g_ref[0])).astype(o_ref.dtype)


def _rglru(h3, cw, cb, wa, wi, ba, bi, lam, width, rows=512):
    bsz, t_len, _ = h3.shape
    rows = min(rows, t_len)
    hb = rows // V7X_SUBLANES
    vec = lambda v: v.reshape(1, width)
    full = lambda shape: pl.BlockSpec(shape, lambda b, c: (0,) * len(shape))
    return pl.pallas_call(
        _rg_kernel,
        grid=(bsz, t_len // rows),
        in_specs=[
            pl.BlockSpec((1, V7X_SUBLANES, width), lambda b, c: (b, jnp.maximum(c * hb - 1, 0), 0)),
            pl.BlockSpec((1, rows, width), lambda b, c: (b, c, 0)),
            pl.BlockSpec((1, rows, width), lambda b, c: (b, c, 1)),
            full((RG_CONV, width)), full((1, width)),
            full(wa.shape), full(wi.shape),
            full((1, width)), full((1, width)), full((1, width)),
        ],
        out_specs=pl.BlockSpec((1, rows, width), lambda b, c: (b, c, 0)),
        out_shape=jax.ShapeDtypeStruct((bsz, t_len, width), BF16),
        scratch_shapes=[pltpu.VMEM((V7X_SUBLANES, width), F32)],
        compiler_params=_params(("arbitrary", "arbitrary")),
        name="rglru",
    )(h3, h3, h3, cw, vec(cb), wa.astype(BF16), wi.astype(BF16), vec(ba), vec(bi), vec(lam))


def _segment_cumsum(v, axis, seg):
    pos = lax.broadcasted_iota(jnp.int32, v.shape, axis) % seg
    s = 1
    while s < seg:
        v = v + jnp.where(pos >= s, pltpu.roll(v, s, axis), 0.0)
        s *= 2
    return v


def _l2_normalize(v):
    return v * lax.rsqrt(jnp.sum(v * v, axis=-1, keepdims=True) + 1e-6)


def _dn_kernel(alog_ref, dtb_ref,
               qh_ref, q_ref, kh_ref, k_ref, vh_ref, v_ref, z_ref, sm_ref, smt_ref,
               cwq_ref, cwk_ref, cwv_ref, nw_ref, o_ref, state_ref, *, n_heads):
    c = pl.program_id(2)
    rows = q_ref.shape[1]
    dk = DN_HEAD_DIM
    heads_here = q_ref.shape[2] // dk
    ch = DN_CHUNK

    @pl.when(c == 0)
    def _():
        state_ref[...] = jnp.zeros_like(state_ref)

    def conv_silu(h_ref, x_ref, w_ref):
        halo = jnp.where(c > 0, h_ref[0], 0.0)
        return _silu(_causal_conv(halo, x_ref[0], w_ref[...]))

    q_all = conv_silu(qh_ref, q_ref, cwq_ref)
    k_all = conv_silu(kh_ref, k_ref, cwk_ref)
    v_all = conv_silu(vh_ref, v_ref, cwv_ref)
    sm = sm_ref[0]
    lane = lax.broadcasted_iota(jnp.int32, sm.shape, 1)
    ri = lax.broadcasted_iota(jnp.int32, (ch, ch), 0)
    ci = lax.broadcasted_iota(jnp.int32, (ch, ch), 1)
    incl = ri >= ci
    strict = ri > ci
    eye = (ri == ci).astype(F32)
    for j in range(heads_here):
        hd = pl.program_id(1) * heads_here + j
        hs = slice(j * dk, (j + 1) * dk)
        _dn_head(hd, q_all[:, hs], k_all[:, hs], v_all[:, hs], z_ref, sm, lane, smt_ref,
                 alog_ref, dtb_ref, nw_ref, o_ref, state_ref, j, hs, incl, strict, eye, n_heads)


def _dn_head(hd, q, k, v, z_ref, sm, lane, smt_ref, alog_ref, dtb_ref, nw_ref, o_ref, state_ref,
             j, hs, incl, strict, eye, n_heads):
    rows, dk = q.shape
    ch = DN_CHUNK
    q = _l2_normalize(q) * (dk ** -0.5)
    k = _l2_normalize(k)
    a_scale = -jnp.exp(jnp.full((1, 1), alog_ref[hd], F32))
    dtb = dtb_ref[hd]
    a_col = jnp.sum(jnp.where(lane == hd, sm, 0.0), axis=1, keepdims=True)
    b_col = jnp.sum(jnp.where(lane == hd + n_heads, sm, 0.0), axis=1, keepdims=True)
    beta = _sigmoid(b_col)
    gc_col = _segment_cumsum(a_scale * _softplus(a_col + dtb), 0, ch)
    a_row = smt_ref[pl.ds(hd, 1), :]
    gc_row = _segment_cumsum(a_scale * _softplus(a_row + dtb), 1, ch)

    state = state_ref[j]
    outs = []
    for n in range(rows // ch):
        sl = slice(n * ch, (n + 1) * ch)
        qn, kn, vn = q[sl], k[sl], v[sl]
        gcc = gc_col[sl]
        gcr = gc_row[:, sl]
        bn = beta[sl]
        decay = jnp.where(incl, jnp.exp(jnp.where(incl, gcc - gcr, 0.0)), 0.0)
        kb = kn * bn
        kb16, k16 = kb.astype(BF16), kn.astype(BF16)
        lower = jnp.where(strict, _dot_nt(kb16, k16) * decay, 0.0)
        p = -lower
        t_inv = eye + p
        m = 2
        while m < ch:
            p16 = p.astype(BF16)
            p = _dot(p16, p16)
            t_inv = t_inv + _dot(t_inv.astype(BF16), p.astype(BF16))
            m *= 2
        t16 = t_inv.astype(BF16)
        eg = jnp.exp(gcc)
        u = _dot(t16, (vn * bn).astype(BF16))
        w = _dot(t16, (kb * eg).astype(BF16))
        attn = _dot_nt(qn.astype(BF16), k16) * decay
        qg = qn * eg
        g_end = gcc[ch - 1:ch]
        kg = kn * jnp.exp(g_end - gcc)
        s16 = state.astype(BF16)
        v_new = u - _dot(w.astype(BF16), s16)
        vn16 = v_new.astype(BF16)
        outs.append(_dot(qg.astype(BF16), s16) + _dot(attn.astype(BF16), vn16))
        state = state * jnp.exp(g_end) + _dot_tn(kg.astype(BF16), vn16)
    state_ref[j] = state
    o = jnp.concatenate(outs, axis=0) if len(outs) > 1 else outs[0]
    o = o * lax.rsqrt(jnp.mean(o * o, axis=-1, keepdims=True) + 1e-6) * nw_ref[...]
    o_ref[0, :, hs] = (o * _silu(z_ref[0, :, hs])).astype(o_ref.dtype)


DN_HEADS_PER_STEP = 4


def _deltanet(h3, small_t, conv_w, a_log, dt_bias, norm_w, *, n_heads, col0, small_col, rows=256):
    bsz, t_len, _ = h3.shape
    rows = min(rows, t_len)
    hps = DN_HEADS_PER_STEP
    dk = DN_HEAD_DIM
    wblk = hps * dk
    hb = rows // V7X_SUBLANES
    nblk = t_len // rows
    cb = col0 // wblk
    per = n_heads // hps
    smb = small_col // 256

    def halo(off):
        return pl.BlockSpec((1, V7X_SUBLANES, wblk),
                            lambda b, h, c: (b, jnp.maximum(c * hb - 1, 0), cb + off * per + h))

    def cur(off):
        return pl.BlockSpec((1, rows, wblk), lambda b, h, c: (b, c, cb + off * per + h))

    def convw(off):
        return pl.BlockSpec((DN_CONV, wblk), lambda b, h, c: (0, off * per + h))

    smem = pl.BlockSpec(memory_space=pltpu.SMEM)
    return pl.pallas_call(
        functools.partial(_dn_kernel, n_heads=n_heads),
        grid=(bsz, per, nblk),
        in_specs=[
            smem, smem,
            halo(0), cur(0), halo(1), cur(1), halo(2), cur(2), cur(3),
            pl.BlockSpec((1, rows, 256), lambda b, h, c: (b, c, smb)),
            pl.BlockSpec((small_t.shape[0], rows), lambda b, h, c: (0, b * nblk + c)),
            convw(0), convw(1), convw(2),
            pl.BlockSpec((1, dk), lambda b, h, c: (0, 0)),
        ],
        out_specs=pl.BlockSpec((1, rows, wblk), lambda b, h, c: (b, c, h)),
        out_shape=jax.ShapeDtypeStruct((bsz, t_len, n_heads * dk), BF16),
        scratch_shapes=[pltpu.VMEM((hps, dk, dk), F32)],
        compiler_params=_params(("arbitrary", "arbitrary", "arbitrary")),
        name="deltanet",
    )(a_log, dt_bias, h3, h3, h3, h3, h3, h3, h3, h3, small_t,
      conv_w, conv_w, conv_w, norm_w.reshape(1, dk))


NSA_QB = 128
NSA_TK = 512
NSA_BIAS_TILES = -(-(REL_MAX_DIST + NSA_QB - 1) // NSA_QB) + 1


def _rel_bucket(dist):
    n = jnp.maximum(dist, 0)
    exact = REL_BUCKETS // 2
    big = exact + (jnp.log(jnp.maximum(n, 1).astype(jnp.float32) / exact)
                   / math.log(REL_MAX_DIST / exact) * (REL_BUCKETS - exact)).astype(jnp.int32)
    return jnp.where(n < exact, n, jnp.minimum(big, REL_BUCKETS - 1))


def _bias_kernel(rb_ref, bk_ref, o_ref):
    bk = bk_ref[...]
    for h in range(o_ref.shape[0]):
        acc = jnp.zeros(bk.shape, F32)
        for b in range(REL_BUCKETS):
            acc = jnp.where(bk == b, rb_ref[b, h], acc)
        o_ref[h] = acc


def _bias_table(rel_bias, buckets, rows=256):
    m, w = buckets.shape
    n_heads = rel_bias.shape[1]
    rows = min(rows, m)
    return pl.pallas_call(
        _bias_kernel,
        grid=(m // rows,),
        in_specs=[pl.BlockSpec(memory_space=pltpu.SMEM),
                  pl.BlockSpec((rows, w), lambda i: (i, 0))],
        out_specs=pl.BlockSpec((n_heads, rows, w), lambda i: (0, i, 0)),
        out_shape=jax.ShapeDtypeStruct((n_heads, m, w), F32),
        compiler_params=_params(("arbitrary",)),
        name="nsa_bias_table",
    )(rel_bias, buckets)


def _cmp_kernel(z_ref, pe_ref, w1_ref, w2_ref, o_ref):
    z = z_ref[0, 0]
    r, half = z.shape
    w1 = w1_ref[0]
    top = _dot(z, w1[:half])
    bot = _dot(z, w1[half:])
    bias = _dot(pe_ref[0], w1)[0:1]
    pre = top + pltpu.roll(bot, r - 1, 0) + bias
    o_ref[0, 0] = _dot(_gelu_tanh(pre).astype(BF16), w2_ref[0]).astype(o_ref.dtype)


def _nsa_compress(z, pe, w1, w2):
    bsz, _, r, half = z.shape
    d = w2.shape[-1]
    return pl.pallas_call(
        _cmp_kernel,
        grid=(bsz, 4),
        in_specs=[
            pl.BlockSpec((1, 1, r, half), lambda b, j: (b, j, 0, 0)),
            pl.BlockSpec((1, V7X_SUBLANES, 2 * half), lambda b, j: (j // 2, 0, 0)),
            pl.BlockSpec((1, 2 * half, w1.shape[-1]), lambda b, j: (j // 2, 0, 0)),
            pl.BlockSpec((1, w2.shape[1], d), lambda b, j: (j // 2, 0, 0)),
        ],
        out_specs=pl.BlockSpec((1, 1, r, d), lambda b, j: (b, j, 0, 0)),
        out_shape=jax.ShapeDtypeStruct((bsz, 4, r, d), BF16),
        compiler_params=_params(("arbitrary", "arbitrary")),
        name="nsa_compress",
    )(z, pe, w1, w2)


def _masked_softmax(logits, mask):
    logits = jnp.where(mask, logits, NEG)
    m = jnp.max(logits, axis=-1, keepdims=True)
    e = jnp.where(mask, jnp.exp(logits - m), 0.0)
    return e / jnp.maximum(jnp.sum(e, axis=-1, keepdims=True), 1e-30)


def _nsa_kernel(q_ref, sm_ref, kc_ref, vc_ref, bc_ref, bt_ref, ovt_ref,
                ks_ref, vs_ref, kw_ref, vw_ref, o_ref, *, n_cmp, n_top, gate_col):
    g = pl.program_id(1)
    qb = pl.program_id(2)
    qr = NSA_QB
    d = NSA_HEAD_DIM
    hpg = q_ref.shape[2] // d
    s0 = qb * qr
    q = q_ref[0] * (d ** -0.5)
    q4 = jnp.concatenate([q[:, h * d:(h + 1) * d] for h in range(hpg)], axis=0).astype(BF16)
    t_col = s0 + lax.broadcasted_iota(jnp.int32, (qr, 1), 0)
    t4 = jnp.concatenate([t_col] * hpg, axis=0)

    r = kc_ref.shape[2]
    n_sel = ovt_ref.shape[0]
    sc = _dot_nt(q4, kc_ref[0, 0]) + bc_ref[...].reshape(hpg * qr, r)
    n_row = lax.broadcasted_iota(jnp.int32, (1, r), 1)
    cmp_end = n_row * NSA_CMP_STRIDE + (NSA_CMP_BLOCK - 1)
    p4 = _masked_softmax(sc, (cmp_end <= t4) & (n_row < n_cmp))
    o_cmp = _dot(p4.astype(BF16), vc_ref[0, 0])

    p_sum = p4[0:qr]
    for h in range(1, hpg):
        p_sum = p_sum + p4[h * qr:(h + 1) * qr]
    p_hi = p_sum.astype(BF16)
    p_lo = (p_sum - p_hi.astype(F32)).astype(BF16)
    ovt = ovt_ref[...]
    imp = _dot_nt(ovt, p_hi) + _dot_nt(ovt, p_lo)
    j_col = lax.broadcasted_iota(jnp.int32, (n_sel, 1), 0)
    t_row = s0 + lax.broadcasted_iota(jnp.int32, (1, qr), 1)
    q_blk = t_row // NSA_SEL_BLOCK
    forced = (j_col == 0) | (j_col == q_blk) | (j_col == q_blk - 1)
    score = jnp.where(j_col * NSA_SEL_BLOCK <= t_row,
                      jnp.where(forced, FORCE_SCORE, imp), -FORCE_SCORE)
    rank = jnp.zeros((n_sel, qr), jnp.int32)
    for k in range(n_sel):
        row = score[k:k + 1]
        ahead = (row > score) | ((row == score) & (j_col > k))
        rank = rank + ahead.astype(jnp.int32)
    sel_t = (rank < n_top).astype(BF16)

    tk = NSA_TK
    key_lane = lax.broadcasted_iota(jnp.int32, (1, tk), 1)
    nb = bt_ref.shape[0]

    def sel_step(j, carry):
        m_run, l_run, acc = carry
        k0 = pl.multiple_of(j * tk, tk)
        kt = ks_ref[0, 0, pl.ds(k0, tk), :]
        vt = vs_ref[0, 0, pl.ds(k0, tk), :]
        bias = jnp.concatenate(
            [bt_ref[jnp.clip(qb - (tk // qr) * j - mm, 0, nb - 1)].reshape(hpg * qr, qr)
             for mm in range(tk // qr)], axis=1)
        s = _dot_nt(q4, kt) + bias
        key = k0 + key_lane
        expand = (j_col == key // NSA_SEL_BLOCK).astype(BF16)
        chosen = _dot_tn(sel_t, expand)
        mask = (chosen > 0.5) & (key <= t_col)
        mask4 = jnp.concatenate([mask] * hpg, axis=0)
        s = jnp.where(mask4, s, NEG)
        m_new = jnp.maximum(m_run, jnp.max(s, axis=-1, keepdims=True))
        scale = jnp.exp(m_run - m_new)
        e = jnp.where(mask4, jnp.exp(s - m_new), 0.0)
        l_new = l_run * scale + jnp.sum(e, axis=-1, keepdims=True)
        acc = acc * scale + _dot(e.astype(BF16), vt)
        return m_new, l_new, acc

    init = (jnp.full((hpg * qr, 1), NEG, F32), jnp.zeros((hpg * qr, 1), F32),
            jnp.zeros((hpg * qr, d), F32))
    _, l_sel, acc_sel = lax.fori_loop(0, (s0 + qr - 1) // tk + 1, sel_step, init)
    o_sel = acc_sel / jnp.maximum(l_sel, 1e-30)

    n_wt = NSA_WINDOW // qr + 1
    kws, vws, biases, masks = [], [], [], []
    i_col = lax.broadcasted_iota(jnp.int32, (qr, 1), 0)
    j_row = lax.broadcasted_iota(jnp.int32, (1, qr), 1)
    for mm in range(n_wt):
        k0 = s0 - NSA_WINDOW + mm * qr
        k0c = pl.multiple_of(jnp.maximum(k0, 0), qr)
        kws.append(kw_ref[0, 0, pl.ds(k0c, qr), :])
        vws.append(vw_ref[0, 0, pl.ds(k0c, qr), :])
        biases.append(bt_ref[n_wt - 1 - mm].reshape(hpg * qr, qr))
        dist = (n_wt - 1 - mm) * qr + i_col - j_row
        masks.append((dist >= 0) & (dist < NSA_WINDOW) & (k0 >= 0))
    sw = _dot_nt(q4, jnp.concatenate(kws, axis=0)) + jnp.concatenate(biases, axis=1)
    mw = jnp.concatenate(masks, axis=1)
    pw = _masked_softmax(sw, jnp.concatenate([mw] * hpg, axis=0))
    o_win = _dot(pw.astype(BF16), jnp.concatenate(vws, axis=0))

    sm = sm_ref[0]
    lane = lax.broadcasted_iota(jnp.int32, sm.shape, 1)

    def gate(h, branch):
        col = gate_col + 3 * (g * hpg + h) + branch
        return _sigmoid(jnp.sum(jnp.where(lane == col, sm, 0.0), axis=1, keepdims=True))

    outs = []
    for h in range(hpg):
        sl = slice(h * qr, (h + 1) * qr)
        outs.append(gate(h, 0) * o_cmp[sl] + gate(h, 1) * o_sel[sl] + gate(h, 2) * o_win[sl])
    o_ref[0] = jnp.concatenate(outs, axis=1).astype(o_ref.dtype)


def _nsa_attention(h3, kv_t, cmp_kv, bias_c, bias_t, ovt, *, q_col, small_col, gate_col, n_cmp):
    bsz, t_len, _ = h3.shape
    g_n = NSA_KV_GROUPS
    d = NSA_HEAD_DIM
    n_heads = bias_c.shape[0]
    hpg = n_heads // g_n
    qw = hpg * d
    r = cmp_kv.shape[2]
    n_sel = t_len // NSA_SEL_BLOCK
    nb = bias_t.shape[0]
    kv_spec = lambda part: pl.BlockSpec((1, 1, t_len, d), lambda b, g, q: (b, 2 * part + g, 0, 0))
    return pl.pallas_call(
        functools.partial(_nsa_kernel, n_cmp=n_cmp, n_top=min(NSA_TOP_BLOCKS, n_sel),
                          gate_col=gate_col),
        grid=(bsz, g_n, t_len // NSA_QB),
        in_specs=[
            pl.BlockSpec((1, NSA_QB, qw), lambda b, g, q: (b, q, q_col // qw + g)),
            pl.BlockSpec((1, NSA_QB, 256), lambda b, g, q: (b, q, small_col // 256)),
            pl.BlockSpec((1, 1, r, d), lambda b, g, q: (b, g, 0, 0)),
            pl.BlockSpec((1, 1, r, d), lambda b, g, q: (b, g_n + g, 0, 0)),
            pl.BlockSpec((hpg, NSA_QB, r), lambda b, g, q: (g, q, 0)),
            pl.BlockSpec((nb, hpg, NSA_QB, NSA_QB), lambda b, g, q: (0, g, 0, 0)),
            pl.BlockSpec((n_sel, r), lambda b, g, q: (0, 0)),
            kv_spec(2), kv_spec(3), kv_spec(4), kv_spec(5),
        ],
        out_specs=pl.BlockSpec((1, NSA_QB, qw), lambda b, g, q: (b, q, g)),
        out_shape=jax.ShapeDtypeStruct((bsz, t_len, n_heads * d), BF16),
        compiler_params=_params(("arbitrary", "arbitrary", "arbitrary")),
        name="nsa_attention",
    )(h3, h3, cmp_kv, cmp_kv, bias_c, bias_t, ovt, kv_t, kv_t, kv_t, kv_t)


def _outproj_kernel(x_ref, ya_ref, yb_ref, yc_ref, wa_ref, wb_ref, wc_ref, g_ref, b_ref, o_ref,
                    *, alpha):
    mix = _dot(ya_ref[...], wa_ref[...]) + _dot(yb_ref[...], wb_ref[...]) + _dot(yc_ref[...], wc_ref[...])
    o_ref[...] = _layer_norm(alpha * x_ref[...] + mix, g_ref[...], b_ref[...])


def _outproj_ln(x2d, ya, yb, yc, w_out, g, b, alpha, tm=256):
    n, d = x2d.shape
    wa_n, wb_n, wc_n = ya.shape[1], yb.shape[1], yc.shape[1]
    w16 = w_out.astype(BF16)
    row = lambda w: pl.BlockSpec((tm, w), lambda i: (i, 0))
    full = lambda r: pl.BlockSpec((r, d), lambda i: (0, 0))
    return pl.pallas_call(
        functools.partial(_outproj_kernel, alpha=alpha),
        grid=(n // tm,),
        in_specs=[row(d), row(wa_n), row(wb_n), row(wc_n),
                  full(wa_n), full(wb_n), full(wc_n), full(1), full(1)],
        out_specs=row(d),
        out_shape=jax.ShapeDtypeStruct((n, d), F32),
        compiler_params=_params(("arbitrary",)),
        name="outproj_ln",
    )(x2d, ya, yb, yc, w16[:wa_n], w16[wa_n:wa_n + wb_n], w16[wa_n + wb_n:],
      g.reshape(1, d), b.reshape(1, d))


def _mm_kernel(x_ref, w_ref, o_ref):
    o_ref[...] = _dot(x_ref[...].astype(BF16), w_ref[...]).astype(o_ref.dtype)


def _matmul(x2d, w16, out_dtype, tm=256):
    n, d = x2d.shape
    cols = w16.shape[1]
    tm = min(tm, n)
    return pl.pallas_call(
        _mm_kernel,
        grid=(n // tm,),
        in_specs=[pl.BlockSpec((tm, d), lambda i: (i, 0)),
                  pl.BlockSpec((d, cols), lambda i: (0, 0))],
        out_specs=pl.BlockSpec((tm, cols), lambda i: (i, 0)),
        out_shape=jax.ShapeDtypeStruct((n, cols), out_dtype),
        compiler_params=_params(("arbitrary",)),
        name="mem_kv_proj",
    )(x2d, w16)


def _xattn_kernel(x_ref, kv_ref, wq_ref, wo_ref, g_ref, b_ref, o_ref, o16_ref, *, alpha):
    x = x_ref[...]
    q = _dot(x.astype(BF16), wq_ref[...])
    kv = kv_ref[0]
    width = wq_ref.shape[1]
    hd = XA_HEAD_DIM
    outs = []
    for h in range(width // hd):
        qh = q[:, h * hd:(h + 1) * hd].astype(BF16)
        kh = kv[:, h * hd:(h + 1) * hd]
        vh = kv[:, width + h * hd:width + (h + 1) * hd]
        logits = _dot_nt(qh, kh) * (hd ** -0.5)
        m = jnp.max(logits, axis=-1, keepdims=True)
        e = jnp.exp(logits - m)
        p = e / jnp.sum(e, axis=-1, keepdims=True)
        outs.append(_dot(p.astype(BF16), vh))
    o = jnp.concatenate(outs, axis=1).astype(BF16)
    y = _layer_norm(alpha * x + _dot(o, wo_ref[...]), g_ref[...], b_ref[...])
    o_ref[...] = y
    o16_ref[...] = y.astype(BF16)


def _cross_attention_ln(x2d, kv, wq, wo, g, b, alpha, t_len, tm=256):
    n, d = x2d.shape
    width = wq.shape[1]
    m_len = kv.shape[1]
    per_b = t_len // tm
    return pl.pallas_call(
        functools.partial(_xattn_kernel, alpha=alpha),
        grid=(n // tm,),
        in_specs=[
            pl.BlockSpec((tm, d), lambda i: (i, 0)),
            pl.BlockSpec((1, m_len, 2 * width), lambda i: (i // per_b, 0, 0)),
            pl.BlockSpec((d, width), lambda i: (0, 0)),
            pl.BlockSpec((width, d), lambda i: (0, 0)),
            pl.BlockSpec((1, d), lambda i: (0, 0)),
            pl.BlockSpec((1, d), lambda i: (0, 0)),
        ],
        out_specs=[pl.BlockSpec((tm, d), lambda i: (i, 0)), pl.BlockSpec((tm, d), lambda i: (i, 0))],
        out_shape=[jax.ShapeDtypeStruct((n, d), F32), jax.ShapeDtypeStruct((n, d), BF16)],
        compiler_params=_params(("arbitrary",)),
        name="cross_attention_ln",
    )(x2d, kv, wq.astype(BF16), wo.astype(BF16), g.reshape(1, d), b.reshape(1, d))


def _rank_rows(v, idx_col):
    rank = jnp.zeros(v.shape, jnp.int32)
    for k in range(v.shape[0]):
        row = v[k:k + 1]
        rank = rank + ((row > v) | ((row == v) & (idx_col > k))).astype(jnp.int32)
    return rank


def _router_kernel(x_ref, w_ref, b_ref, e_ref, g_ref, cnt_ref):
    n_exp = w_ref.shape[0]
    gsz = n_exp // N_GROUPS
    logits = lax.dot_general(w_ref[...], x_ref[...], (((1,), (1,)), ((), ())),
                             preferred_element_type=F32, precision=lax.Precision.HIGHEST)
    scores = _sigmoid(logits)
    choice = scores + b_ref[...]
    tm = scores.shape[1]
    sub = lax.broadcasted_iota(jnp.int32, (gsz, 1), 0)
    grp_rows = []
    for gi in range(N_GROUPS):
        slab = choice[gi * gsz:(gi + 1) * gsz]
        m1 = jnp.max(slab, axis=0, keepdims=True)
        first = jnp.min(jnp.where(slab == m1, sub, gsz), axis=0, keepdims=True)
        m2 = jnp.max(jnp.where(sub == first, NEG, slab), axis=0, keepdims=True)
        grp_rows.append(m1 + m2)
    grp = jnp.concatenate(grp_rows, axis=0)
    g_idx = lax.broadcasted_iota(jnp.int32, (N_GROUPS, 1), 0)
    grp_ok = _rank_rows(grp, g_idx) < TOPK_GROUPS
    ok = jnp.concatenate([jnp.broadcast_to(grp_ok[gi:gi + 1], (gsz, tm)) for gi in range(N_GROUPS)], axis=0)
    e_idx = lax.broadcasted_iota(jnp.int32, (n_exp, 1), 0)
    rank = _rank_rows(jnp.where(ok, choice, -1e9), e_idx)
    denom = jnp.sum(jnp.where(rank < TOP_K, scores, 0.0), axis=0, keepdims=True)
    e_rows, g_rows = [], []
    for k in range(TOP_K):
        hit = rank == k
        e_rows.append(jnp.sum(jnp.where(hit, e_idx, 0), axis=0, keepdims=True))
        g_rows.append(jnp.sum(jnp.where(hit, scores, 0.0), axis=0, keepdims=True) / denom * ROUTED_SCALE)
    e_ref[...] = jnp.concatenate(e_rows, axis=0)
    g_ref[...] = jnp.concatenate(g_rows, axis=0)

    @pl.when(pl.program_id(0) == 0)
    def _():
        cnt_ref[...] = jnp.zeros_like(cnt_ref)

    cnt_ref[...] += jnp.sum((rank < TOP_K).astype(jnp.int32), axis=1, keepdims=True)


def _router(x2d, w_router, bias, tm=512):
    n, d = x2d.shape
    n_exp = w_router.shape[1]
    return pl.pallas_call(
        _router_kernel,
        grid=(n // tm,),
        in_specs=[pl.BlockSpec((tm, d), lambda i: (i, 0)),
                  pl.BlockSpec((n_exp, d), lambda i: (0, 0)),
                  pl.BlockSpec((n_exp, 1), lambda i: (0, 0))],
        out_specs=[pl.BlockSpec((TOP_K, tm), lambda i: (0, i)), pl.BlockSpec((TOP_K, tm), lambda i: (0, i)),
                   pl.BlockSpec((n_exp, 1), lambda i: (0, 0))],
        out_shape=[jax.ShapeDtypeStruct((TOP_K, n), jnp.int32), jax.ShapeDtypeStruct((TOP_K, n), F32),
                   jax.ShapeDtypeStruct((n_exp, 1), jnp.int32)],
        compiler_params=_params(("arbitrary",)),
        name="moe_router",
    )(x2d, w_router.T, bias.reshape(n_exp, 1))


def _expert_kernel(e_ref, b_ref, lo_ref, hi_ref, first_ref,
                   tok_ref, dst_ref, tok_next_ref, x_hbm, wt_ref, wg_ref, wu_ref, wd_ref, y_hbm,
                   xbuf, obuf, wg16, wu16, wd16, gsem, ssem):
    i = pl.program_id(0)
    n_items = pl.num_programs(0)
    slot = i % 2
    rows = wt_ref.shape[0]
    cpr = xbuf.shape[1] // rows
    lo, hi = lo_ref[i], hi_ref[i]

    def slab(ref, r):
        return ref.at[pl.ds(pl.multiple_of(r * cpr, cpr), cpr)]

    def start_gather(idx_ref, r0, r1, s):
        def body(r, _):
            pltpu.make_async_copy(slab(x_hbm, idx_ref[0, 0, r]), slab(xbuf.at[s], r), gsem.at[s]).start()
            return 0

        lax.fori_loop(r0, r1, body, 0)

    def wait_gather(n, s):
        def body(r, _):
            pltpu.make_async_copy(slab(x_hbm, 0), slab(xbuf.at[s], 0), gsem.at[s]).wait()
            return 0

        lax.fori_loop(0, n, body, 0)

    def wait_scatter(n, s):
        def body(r, _):
            pltpu.make_async_copy(slab(obuf.at[s], 0), slab(y_hbm, 0), ssem.at[s]).wait()
            return 0

        lax.fori_loop(0, n, body, 0)

    @pl.when(i == 0)
    def _():
        xbuf[...] = jnp.zeros_like(xbuf)
        start_gather(tok_ref, lo, hi, 0)

    @pl.when(i + 1 < n_items)
    def _():
        start_gather(tok_next_ref, lo_ref[i + 1], hi_ref[i + 1], 1 - slot)

    @pl.when(first_ref[i] == 1)
    def _():
        wg16[...] = wg_ref[...].astype(BF16)
        wu16[...] = wu_ref[...].astype(BF16)
        wd16[...] = wd_ref[...].astype(BF16)

    wait_gather(hi - lo, slot)

    @pl.when(i >= 2)
    def _():
        wait_scatter(hi_ref[i - 2] - lo_ref[i - 2], slot)

    @pl.when(hi > lo)
    def _():
        xb = jnp.concatenate([xbuf[slot, pl.ds(c, rows, stride=cpr), :] for c in range(cpr)], axis=1)
        xb = xb.astype(BF16)
        hid = _silu(_dot(xb, wg16[...])) * _dot(xb, wu16[...])
        out = _dot(hid.astype(BF16), wd16[...]) * wt_ref[...]
        for c in range(cpr):
            obuf[slot, pl.ds(c, rows, stride=cpr), :] = out[:, c * V7X_LANES:(c + 1) * V7X_LANES]

        def scatter(r, _):
            pltpu.make_async_copy(slab(obuf.at[slot], r), slab(y_hbm, dst_ref[0, 0, r]), ssem.at[slot]).start()
            return 0

        lax.fori_loop(lo, hi, scatter, 0)

    @pl.when(i == n_items - 1)
    def _():
        @pl.when(i >= 1)
        def _():
            wait_scatter(hi_ref[i - 1] - lo_ref[i - 1], 1 - slot)

        wait_scatter(hi - lo, slot)


def _experts(x2d, plan, w_gate, w_up, w_down, layer):
    item_e, item_b, item_lo, item_hi, item_first, row_tok, row_dst, row_w = plan
    n, d = x2d.shape
    n_items = item_e.shape[0]
    dff = w_gate.shape[3]
    rows = MOE_ROWS
    n_blk = row_tok.shape[0] // rows
    cpr = d // V7X_LANES
    wspec = lambda r, c: pl.BlockSpec((None, None, r, c), lambda i, e, *_: (layer, e[i], 0, 0))
    cur = lambda i, e, b, *_: (b[i], 0, 0)
    nxt = lambda i, e, b, *_: (b[jnp.minimum(i + 1, n_items - 1)], 0, 0)
    grid_spec = pltpu.PrefetchScalarGridSpec(
        num_scalar_prefetch=5,
        grid=(n_items,),
        in_specs=[
            pl.BlockSpec((1, 1, rows), cur, memory_space=pltpu.SMEM),
            pl.BlockSpec((1, 1, rows), cur, memory_space=pltpu.SMEM),
            pl.BlockSpec((1, 1, rows), nxt, memory_space=pltpu.SMEM),
            pl.BlockSpec(memory_space=pl.ANY),
            pl.BlockSpec((rows, 1), lambda i, e, b, *_: (b[i], 0)),
            wspec(d, dff), wspec(d, dff), wspec(dff, d),
        ],
        out_specs=pl.BlockSpec(memory_space=pl.ANY),
        scratch_shapes=[
            pltpu.VMEM((2, rows * cpr, V7X_LANES), F32), pltpu.VMEM((2, rows * cpr, V7X_LANES), F32),
            pltpu.VMEM((d, dff), BF16), pltpu.VMEM((d, dff), BF16), pltpu.VMEM((dff, d), BF16),
            pltpu.SemaphoreType.DMA((2,)), pltpu.SemaphoreType.DMA((2,)),
        ],
    )
    tok3 = row_tok.reshape(n_blk, 1, rows)
    return pl.pallas_call(
        _expert_kernel,
        grid_spec=grid_spec,
        out_shape=jax.ShapeDtypeStruct((n * TOP_K * cpr, V7X_LANES), F32),
        compiler_params=_params(("arbitrary",)),
        name="moe_experts",
    )(item_e, item_b, item_lo, item_hi, item_first,
      tok3, row_dst.reshape(n_blk, 1, rows), tok3,
      x2d.reshape(n * cpr, V7X_LANES), row_w.reshape(n_blk * rows, 1), w_gate, w_up, w_down)


def _combine_kernel(x_ref, x16_ref, y_ref, wg_ref, wu_ref, wd_ref, g_ref, b_ref, o_ref, acc_ref,
                    *, alpha):
    x16 = x16_ref[...]
    shared = _dot((_silu(_dot(x16, wg_ref[...])) * _dot(x16, wu_ref[...])).astype(BF16), wd_ref[...])
    tm, d = x_ref.shape
    cpr = d // V7X_LANES
    tot = y_ref[0]
    for k in range(1, TOP_K):
        tot = tot + y_ref[k]
    acc_ref[...] = tot
    routed = jnp.concatenate([acc_ref[pl.ds(c, tm, stride=cpr), :] for c in range(cpr)], axis=1)
    o_ref[...] = _layer_norm(alpha * x_ref[...] + (routed + shared), g_ref[...], b_ref[...])


def _moe_combine_ln(x2d, x16, y8, ws_gate, ws_up, ws_down, g, b, alpha, tm=128):
    n, d = x2d.shape
    dff = ws_gate.shape[1]
    cpr = d // V7X_LANES
    row = lambda w: pl.BlockSpec((tm, w), lambda i: (i, 0))
    full = lambda r, c: pl.BlockSpec((r, c), lambda i: (0, 0))
    return pl.pallas_call(
        functools.partial(_combine_kernel, alpha=alpha),
        grid=(n // tm,),
        in_specs=[row(d), row(d), pl.BlockSpec((TOP_K, tm * cpr, V7X_LANES), lambda i: (0, i, 0)),
                  full(d, dff), full(d, dff), full(dff, d), full(1, d), full(1, d)],
        out_specs=row(d),
        out_shape=jax.ShapeDtypeStruct((n, d), F32),
        scratch_shapes=[pltpu.VMEM((tm * cpr, V7X_LANES), F32)],
        compiler_params=_params(("arbitrary",)),
        name="moe_combine_ln",
    )(x2d, x16, y8.reshape(TOP_K, n * cpr, V7X_LANES), ws_gate.astype(BF16), ws_up.astype(BF16),
      ws_down.astype(BF16), g.reshape(1, d), b.reshape(1, d))


def _dispatch_plan(top_e, gate, counts):
    n = top_e.shape[1]
    n_exp = counts.shape[0]
    rows = MOE_ROWS
    n_assign = n * TOP_K
    n_items = n_assign // rows + n_exp
    a_iota = jnp.arange(n_assign, dtype=jnp.int32)
    _, row_dst, row_w = lax.sort((top_e.reshape(-1), a_iota, gate.reshape(-1)), num_keys=1)
    row_tok = row_dst % n
    ends = jnp.cumsum(counts)
    starts = ends - counts
    n_blk_e = jnp.where(counts > 0, (ends - 1) // rows - starts // rows + 1, 0)
    item_end = jnp.cumsum(n_blk_e)
    item_start = item_end - n_blk_e
    t = jnp.arange(n_items, dtype=jnp.int32)
    active = t < item_end[-1]
    item_e = jnp.minimum(jnp.sum((item_end[None, :] <= t[:, None]).astype(jnp.int32), axis=1), n_exp - 1)
    onehot = item_e[:, None] == jnp.arange(n_exp, dtype=jnp.int32)[None, :]
    pick = lambda v: jnp.sum(jnp.where(onehot, v[None, :], 0), axis=1)
    s_e, e_e, i_e = pick(starts), pick(ends), pick(item_start)
    item_b = jnp.clip(s_e // rows + (t - i_e), 0, n_assign // rows - 1)
    item_lo = jnp.where(active, jnp.clip(s_e - item_b * rows, 0, rows), 0)
    item_hi = jnp.where(active, jnp.clip(e_e - item_b * rows, 0, rows), 0)
    item_first = (active & (t == i_e)).astype(jnp.int32)
    i32 = lambda v: v.astype(jnp.int32)
    return (i32(item_e), i32(item_b), i32(item_lo), i32(item_hi), item_first,
            i32(row_tok), i32(row_dst), row_w)


def kernel(x, mem, rel_bias, w_in, rg_conv_w, rg_conv_b, rg_wa, rg_ba, rg_wi, rg_bi, rg_lambda,
           dn_conv_w, dn_a_log, dn_dt_bias, dn_norm_w, nsa_pe_k, nsa_pe_v, nsa_phi_k1, nsa_phi_k2,
           nsa_phi_v1, nsa_phi_v2, w_out, ln1_g, ln1_b, xa_wq, xa_wk, xa_wv, xa_wo, ln2_g, ln2_b,
           moe_router, moe_router_bias, moe_w_gate, moe_w_up, moe_w_down, shared_w_gate, shared_w_up,
           shared_w_down, ln3_g, ln3_b):
    bsz, t_len, d_model = x.shape
    depth = w_in.shape[0]
    n_tok = bsz * t_len
    alpha = (2 * depth) ** 0.25
    rg_w = rg_conv_w.shape[2]
    dn_w = dn_conv_w.shape[2] // 3
    dn_heads = dn_a_log.shape[1]
    nsa_heads = rel_bias.shape[1]
    nsa_w = nsa_heads * NSA_HEAD_DIM
    kv_w = NSA_KV_GROUPS * NSA_HEAD_DIM
    n_exp = moe_router.shape[2]

    sizes = (rg_w, rg_w, dn_w, dn_w, dn_w, dn_w, dn_heads, dn_heads, nsa_w, 6 * kv_w, 3 * nsa_heads)
    offs = [0]
    for s in sizes:
        offs.append(offs[-1] + s)
    col = lambda w, i: w[:, :, offs[i]:offs[i + 1]]
    main = offs[6]
    q_col = main
    small_col = q_col + nsa_w
    kv_col = small_col + 256
    gate_col = 2 * dn_heads
    small = jnp.concatenate([col(w_in, 6), col(w_in, 7), col(w_in, 10)], axis=2)
    small = jnp.pad(small, ((0, 0), (0, 0), (0, 256 - small.shape[2])))
    w_slab = jnp.concatenate([w_in[:, :, :main], col(w_in, 8), small, col(w_in, 9)], axis=2).astype(BF16)
    w_small_t = jnp.swapaxes(jnp.concatenate([col(w_in, 6), col(w_in, 7)], axis=2), 1, 2).astype(BF16)

    r_cmp = t_len // NSA_CMP_STRIDE
    n_cmp = (t_len - NSA_CMP_BLOCK) // NSA_CMP_STRIDE + 1
    n_sel = t_len // NSA_SEL_BLOCK
    cmp_start = jnp.arange(r_cmp) * NSA_CMP_STRIDE
    cmp_end = cmp_start + NSA_CMP_BLOCK - 1
    sel_start = jnp.arange(n_sel) * NSA_SEL_BLOCK
    ovt = ((cmp_start[None, :] < sel_start[:, None] + NSA_SEL_BLOCK)
           & (cmp_end[None, :] >= sel_start[:, None])
           & (jnp.arange(r_cmp)[None, :] < n_cmp)).astype(BF16)
    bucket_c = _rel_bucket(jnp.arange(t_len)[:, None] - cmp_end[None, :]).astype(jnp.int32)
    tile = jnp.arange(NSA_BIAS_TILES * NSA_QB)[:, None] - jnp.arange(NSA_QB)[None, :]
    bucket_t = _rel_bucket(tile).astype(jnp.int32)
    bias_c = _bias_table(rel_bias, bucket_c)
    bias_t = _bias_table(rel_bias, bucket_t).reshape(nsa_heads, NSA_BIAS_TILES, NSA_QB, NSA_QB)
    bias_t = jnp.swapaxes(bias_t, 0, 1)

    cmp_in = NSA_CMP_BLOCK * NSA_HEAD_DIM
    x2d = x.reshape(n_tok, d_model)
    mem2d = mem.reshape(-1, d_model)
    for l in range(depth):
        h, small_t = _inproj(x2d, w_slab, w_small_t, l)
        h3 = h.reshape(bsz, t_len, -1)
        y_a = _rglru(h3, rg_conv_w[l], rg_conv_b[l], rg_wa[l], rg_wi[l], rg_ba[l], rg_bi[l],
                     rg_lambda[l], rg_w)
        y_b = _deltanet(h3, small_t, dn_conv_w[l], dn_a_log[l], dn_dt_bias[l], dn_norm_w[l],
                        n_heads=dn_heads, col0=2 * rg_w, small_col=small_col)
        kv_t = h3[:, :, kv_col:].reshape(bsz, t_len, 6 * NSA_KV_GROUPS, NSA_HEAD_DIM)
        kv_t = jnp.swapaxes(kv_t, 1, 2).astype(BF16)
        z = kv_t[:, :2 * NSA_KV_GROUPS].reshape(bsz, 2 * NSA_KV_GROUPS, r_cmp, cmp_in // 2)
        pe = jnp.stack([nsa_pe_k[l], nsa_pe_v[l]]).reshape(2, 1, cmp_in)
        pe = jnp.broadcast_to(pe, (2, V7X_SUBLANES, cmp_in)).astype(BF16)
        w1 = jnp.stack([nsa_phi_k1[l], nsa_phi_v1[l]]).astype(BF16)
        w2 = jnp.stack([nsa_phi_k2[l], nsa_phi_v2[l]]).astype(BF16)
        cmp_kv = _nsa_compress(z, pe, w1, w2)
        y_c = _nsa_attention(h3, kv_t, cmp_kv, bias_c, bias_t, ovt, q_col=q_col,
                             small_col=small_col, gate_col=gate_col, n_cmp=n_cmp)
        x2d = _outproj_ln(x2d, y_a.reshape(n_tok, -1), y_b.reshape(n_tok, -1), y_c.reshape(n_tok, -1),
                          w_out[l], ln1_g[l], ln1_b[l], alpha)
        w_kv = jnp.concatenate([xa_wk[l], xa_wv[l]], axis=1).astype(BF16)
        kv_mem = _matmul(mem2d, w_kv, BF16).reshape(bsz, mem.shape[1], -1)
        x2d, x16 = _cross_attention_ln(x2d, kv_mem, xa_wq[l], xa_wo[l], ln2_g[l], ln2_b[l], alpha, t_len)
        top_e, gate, counts = _router(x2d, moe_router[l], moe_router_bias[l])
        plan = _dispatch_plan(top_e, gate, counts.reshape(-1))
        y8 = _experts(x2d, plan, moe_w_gate, moe_w_up, moe_w_down, l)
        x2d = _moe_combine_ln(x2d, x16, y8, shared_w_gate[l], shared_w_up[l], shared_w_down[l],
                              ln3_g[l], ln3_b[l], alpha)
    return x2d.reshape(bsz, t_len, d_model)
```

```python
import functools
import math

import jax
import jax.numpy as jnp
from jax import lax
from jax.experimental import pallas as pl
from jax.experimental.pallas import tpu as pltpu

F32 = jnp.float32
BF16 = jnp.bfloat16

LN_EPS = 1e-5
NEG = -1e30

RG_BLOCKS = 4
RG_CONV = 4
RG_C = 8.0
DN_HEAD_DIM = 128
DN_CONV = 4
DN_CHUNK = 64
DN_ROWS = 256
NSA_HEAD_DIM = 64
NSA_KV_GROUPS = 2
NSA_CMP_BLOCK = 32
NSA_CMP_STRIDE = 16
NSA_SEL_BLOCK = 64
NSA_TOP_BLOCKS = 16
NSA_WINDOW = 512
FORCE_SCORE = 1e4
REL_BUCKETS = 32
REL_MAX_DIST = 1024
XA_HEADS = 4
XA_HEAD_DIM = 128
TOP_K = 8
N_GROUPS = 8
TOPK_GROUPS = 4
ROUTED_SCALE = 2.5
MOE_ROWS = 256
MOE_DMA_UNROLL = 8

V7X_LANES = 128
V7X_SUBLANES = 8
V7X_VMEM_LIMIT = 56 * 1024 * 1024


def _params(semantics, vmem=V7X_VMEM_LIMIT):
    return pltpu.CompilerParams(dimension_semantics=semantics, vmem_limit_bytes=vmem)


def _sigmoid(x):
    return 1.0 / (1.0 + jnp.exp(-x))


def _softplus(x):
    return jnp.maximum(x, 0.0) + jnp.log1p(jnp.exp(-jnp.abs(x)))


def _silu(x):
    return x * _sigmoid(x)


def _gelu_tanh(x):
    c = math.sqrt(2.0 / math.pi)
    return x * (0.5 * (1.0 + jnp.tanh(c * (x + 0.044715 * (x * x * x)))))


def _layer_norm(v, g, b):
    mu = jnp.mean(v, axis=-1, keepdims=True)
    vc = v - mu
    var = jnp.mean(vc * vc, axis=-1, keepdims=True)
    return vc * lax.rsqrt(var + LN_EPS) * g + b


def _dot(a, b):
    return jnp.dot(a, b, preferred_element_type=F32)


def _dot_nt(a, b):
    return lax.dot_general(a, b, (((1,), (1,)), ((), ())), preferred_element_type=F32)


def _dot_tn(a, b):
    return lax.dot_general(a, b, (((0,), (0,)), ((), ())), preferred_element_type=F32)


def _inproj_kernel(x_ref, w_ref, wt_ref, o_ref, ot_ref, xb_ref):
    @pl.when(pl.program_id(1) == 0)
    def _():
        xb_ref[...] = x_ref[...].astype(BF16)
        ot_ref[...] = _dot_nt(wt_ref[...], xb_ref[...])

    o_ref[...] = _dot(xb_ref[...], w_ref[...])


def _inproj(x2d, w, wt, layer, tm=1024, tn=512):
    n, d = x2d.shape
    cols = w.shape[2]
    return pl.pallas_call(
        _inproj_kernel,
        grid=(n // tm, cols // tn),
        in_specs=[
            pl.BlockSpec((tm, d), lambda i, j: (i, 0)),
            pl.BlockSpec((None, d, tn), lambda i, j: (layer, 0, j)),
            pl.BlockSpec((None, wt.shape[1], d), lambda i, j: (layer, 0, 0)),
        ],
        out_specs=[
            pl.BlockSpec((tm, tn), lambda i, j: (i, j)),
            pl.BlockSpec((wt.shape[1], tm), lambda i, j: (0, i)),
        ],
        out_shape=[
            jax.ShapeDtypeStruct((n, cols), F32),
            jax.ShapeDtypeStruct((wt.shape[1], n), F32),
        ],
        scratch_shapes=[pltpu.VMEM((tm, d), BF16)],
        compiler_params=_params(("arbitrary", "arbitrary")),
        name="inproj",
    )(x2d, w, wt)


def _shifted_rows(halo, x, s):
    cat = jnp.concatenate([halo, x], axis=0)
    return pltpu.roll(cat, s, 0)[V7X_SUBLANES:]


def _causal_conv(halo, x, w):
    k = w.shape[0]
    y = x * w[k - 1:k]
    for s in range(1, k):
        y = y + _shifted_rows(halo, x, s) * w[k - 1 - s:k - s]
    return y


def _rg_kernel(xh_ref, x_ref, g_ref, cw_ref, cb_ref, wa_ref, wi_ref, ba_ref, bi_ref, lam_ref,
               o_ref, h_ref):
    c = pl.program_id(1)
    x = x_ref[0]
    rows, width = x.shape
    blk = width // RG_BLOCKS
    halo = jnp.where(c > 0, xh_ref[0], 0.0)
    xc = _causal_conv(halo, x, cw_ref[...]) + cb_ref[...]
    xcb = xc.astype(BF16)
    r_parts, i_parts = [], []
    for n in range(RG_BLOCKS):
        xg = xcb[:, n * blk:(n + 1) * blk]
        r_parts.append(_dot(xg, wa_ref[n]))
        i_parts.append(_dot(xg, wi_ref[n]))
    r = _sigmoid(jnp.concatenate(r_parts, axis=1) + ba_ref[...])
    gi = _sigmoid(jnp.concatenate(i_parts, axis=1) + bi_ref[...])
    log_a = -RG_C * r * _softplus(-lam_ref[...])
    a = jnp.exp(log_a)
    th = jnp.tanh(log_a)
    b = jnp.sqrt(-2.0 * th / (1.0 - th)) * (gi * xc)
    row = lax.broadcasted_iota(jnp.int32, (rows, 1), 0)
    s = 1
    while s < rows:
        a_s = pltpu.roll(a, s, 0)
        b_s = pltpu.roll(b, s, 0)
        m = row >= s
        b = jnp.where(m, a * b_s + b, b)
        a = jnp.where(m, a * a_s, a)
        s *= 2

    @pl.when(c == 0)
    def _():
        h_ref[...] = jnp.zeros_like(h_ref)

    h = a * h_ref[0:1] + b
    h_ref[...] = jnp.broadcast_to(h[rows - 1:rows], h_ref.shape)
    o_ref[0] = (h * _gelu_tanh(g_ref[0])).astype(o_ref.dtype)


def _rglru(h3, cw, cb, wa, wi, ba, bi, lam, width, rows=512):
    bsz, t_len, _ = h3.shape
    rows = min(rows, t_len)
    hb = rows // V7X_SUBLANES
    vec = lambda v: v.reshape(1, width)
    full = lambda shape: pl.BlockSpec(shape, lambda b, c: (0,) * len(shape))
    return pl.pallas_call(
        _rg_kernel,
        grid=(bsz, t_len // rows),
        in_specs=[
            pl.BlockSpec((1, V7X_SUBLANES, width), lambda b, c: (b, jnp.maximum(c * hb - 1, 0), 0)),
            pl.BlockSpec((1, rows, width), lambda b, c: (b, c, 0)),
            pl.BlockSpec((1, rows, width), lambda b, c: (b, c, 1)),
            full((RG_CONV, width)), full((1, width)),
            full(wa.shape), full(wi.shape),
            full((1, width)), full((1, width)), full((1, width)),
        ],
        out_specs=pl.BlockSpec((1, rows, width), lambda b, c: (b, c, 0)),
        out_shape=jax.ShapeDtypeStruct((bsz, t_len, width), BF16),
        scratch_shapes=[pltpu.VMEM((V7X_SUBLANES, width), F32)],
        compiler_params=_params(("arbitrary", "arbitrary")),
        name="rglru",
    )(h3, h3, h3, cw, vec(cb), wa.astype(BF16), wi.astype(BF16), vec(ba), vec(bi), vec(lam))


def _segment_cumsum(v, axis, seg):
    pos = lax.broadcasted_iota(jnp.int32, v.shape, axis) % seg
    s = 1
    while s < seg:
        v = v + jnp.where(pos >= s, pltpu.roll(v, s, axis), 0.0)
        s *= 2
    return v


def _l2_normalize(v):
    return v * lax.rsqrt(jnp.sum(v * v, axis=-1, keepdims=True) + 1e-6)


def _dn_kernel(alog_ref, dtb_ref,
               qh_ref, q_ref, kh_ref, k_ref, vh_ref, v_ref, z_ref, sm_ref, smt_ref,
               cwq_ref, cwk_ref, cwv_ref, nw_ref, o_ref, state_ref, *, n_heads):
    c = pl.program_id(2)
    rows = q_ref.shape[1]
    dk = DN_HEAD_DIM
    heads_here = q_ref.shape[2] // dk
    ch = DN_CHUNK

    @pl.when(c == 0)
    def _():
        state_ref[...] = jnp.zeros_like(state_ref)

    def conv_silu(h_ref, x_ref, w_ref):
        halo = jnp.where(c > 0, h_ref[0], 0.0)
        return _silu(_causal_conv(halo, x_ref[0], w_ref[...]))

    q_all = conv_silu(qh_ref, q_ref, cwq_ref)
    k_all = conv_silu(kh_ref, k_ref, cwk_ref)
    v_all = conv_silu(vh_ref, v_ref, cwv_ref)
    sm = sm_ref[0]
    lane = lax.broadcasted_iota(jnp.int32, sm.shape, 1)
    ri = lax.broadcasted_iota(jnp.int32, (rows, rows), 0)
    ci = lax.broadcasted_iota(jnp.int32, (rows, rows), 1)
    same_chunk = (ri // ch) == (ci // ch)
    incl = same_chunk & (ri >= ci)
    strict = same_chunk & (ri > ci)
    eye = (ri == ci).astype(F32)
    for j in range(heads_here):
        hd = pl.program_id(1) * heads_here + j
        hs = slice(j * dk, (j + 1) * dk)
        _dn_head(hd, q_all[:, hs], k_all[:, hs], v_all[:, hs], z_ref, sm, lane, smt_ref,
                 alog_ref, dtb_ref, nw_ref, o_ref, state_ref, j, hs, incl, strict, eye, n_heads)


def _dn_head(hd, q, k, v, z_ref, sm, lane, smt_ref, alog_ref, dtb_ref, nw_ref, o_ref, state_ref,
             j, hs, incl, strict, eye, n_heads):
    rows, dk = q.shape
    ch = DN_CHUNK
    q = _l2_normalize(q) * (dk ** -0.5)
    k = _l2_normalize(k)
    a_scale = -jnp.exp(jnp.full((1, 1), alog_ref[hd], F32))
    dtb = dtb_ref[hd]
    a_col = jnp.sum(jnp.where(lane == hd, sm, 0.0), axis=1, keepdims=True)
    b_col = jnp.sum(jnp.where(lane == hd + n_heads, sm, 0.0), axis=1, keepdims=True)
    beta = _sigmoid(b_col)
    gc_col = _segment_cumsum(a_scale * _softplus(a_col + dtb), 0, ch)
    a_row = smt_ref[pl.ds(hd, 1), :]
    gc_row = _segment_cumsum(a_scale * _softplus(a_row + dtb), 1, ch)

    n_ch = rows // ch
    decay = jnp.where(incl, jnp.exp(jnp.where(incl, gc_col - gc_row, 0.0)), 0.0)
    kb = k * beta
    kb16, k16 = kb.astype(BF16), k.astype(BF16)
    lower = jnp.where(strict, _dot_nt(kb16, k16) * decay, 0.0)
    p = -lower
    t_inv = eye + p
    m = 2
    while m < ch:
        p16 = p.astype(BF16)
        p = _dot(p16, p16)
        t_inv = t_inv + _dot(t_inv.astype(BF16), p.astype(BF16))
        m *= 2
    t16 = t_inv.astype(BF16)
    eg = jnp.exp(gc_col)
    u16 = _dot(t16, (v * beta).astype(BF16)).astype(BF16)
    w16 = _dot(t16, (kb * eg).astype(BF16)).astype(BF16)
    attn16 = (_dot_nt(q.astype(BF16), k16) * decay).astype(BF16)
    g_ends = [gc_col[(n + 1) * ch - 1:(n + 1) * ch] for n in range(n_ch)]
    g_end_rows = jnp.concatenate([jnp.broadcast_to(g, (ch, 1)) for g in g_ends], axis=0)
    kg16 = (k * jnp.exp(g_end_rows - gc_col)).astype(BF16)
    qp16 = (q * eg - _dot(attn16, w16)).astype(BF16)
    au = _dot(attn16, u16)
    kws, cs = [], []
    for n in range(n_ch):
        sl = slice(n * ch, (n + 1) * ch)
        kws.append(_dot_tn(kg16[sl], w16[sl]).astype(BF16))
        cs.append(_dot_tn(kg16[sl], u16[sl]))
    state = state_ref[j]
    outs = []
    for n in range(n_ch):
        sl = slice(n * ch, (n + 1) * ch)
        s16 = state.astype(BF16)
        outs.append(_dot(qp16[sl], s16) + au[sl])
        state = state * jnp.exp(g_ends[n]) - _dot(kws[n], s16) + cs[n]
    state_ref[j] = state
    o = jnp.concatenate(outs, axis=0) if len(outs) > 1 else outs[0]
    o = o * lax.rsqrt(jnp.mean(o * o, axis=-1, keepdims=True) + 1e-6) * nw_ref[...]
    o_ref[0, :, hs] = (o * _silu(z_ref[0, :, hs])).astype(o_ref.dtype)


DN_HEADS_PER_STEP = 4


def _deltanet(h3, small_t, conv_w, a_log, dt_bias, norm_w, *, n_heads, col0, small_col, rows=DN_ROWS):
    bsz, t_len, _ = h3.shape
    rows = min(rows, t_len)
    hps = DN_HEADS_PER_STEP
    dk = DN_HEAD_DIM
    wblk = hps * dk
    hb = rows // V7X_SUBLANES
    nblk = t_len // rows
    cb = col0 // wblk
    per = n_heads // hps
    smb = small_col // 256

    def halo(off):
        return pl.BlockSpec((1, V7X_SUBLANES, wblk),
                            lambda b, h, c: (b, jnp.maximum(c * hb - 1, 0), cb + off * per + h))

    def cur(off):
        return pl.BlockSpec((1, rows, wblk), lambda b, h, c: (b, c, cb + off * per + h))

    def convw(off):
        return pl.BlockSpec((DN_CONV, wblk), lambda b, h, c: (0, off * per + h))

    smem = pl.BlockSpec(memory_space=pltpu.SMEM)
    return pl.pallas_call(
        functools.partial(_dn_kernel, n_heads=n_heads),
        grid=(bsz, per, nblk),
        in_specs=[
            smem, smem,
            halo(0), cur(0), halo(1), cur(1), halo(2), cur(2), cur(3),
            pl.BlockSpec((1, rows, 256), lambda b, h, c: (b, c, smb)),
            pl.BlockSpec((small_t.shape[0], rows), lambda b, h, c: (0, b * nblk + c)),
            convw(0), convw(1), convw(2),
            pl.BlockSpec((1, dk), lambda b, h, c: (0, 0)),
        ],
        out_specs=pl.BlockSpec((1, rows, wblk), lambda b, h, c: (b, c, h)),
        out_shape=jax.ShapeDtypeStruct((bsz, t_len, n_heads * dk), BF16),
        scratch_shapes=[pltpu.VMEM((hps, dk, dk), F32)],
        compiler_params=_params(("arbitrary", "arbitrary", "arbitrary")),
        name="deltanet",
    )(a_log, dt_bias, h3, h3, h3, h3, h3, h3, h3, h3, small_t,
      conv_w, conv_w, conv_w, norm_w.reshape(1, dk))


NSA_QB = 128
NSA_TK = 512
NSA_BIAS_TILES = -(-(REL_MAX_DIST + NSA_QB - 1) // NSA_QB) + 1


def _rel_bucket(dist):
    n = jnp.maximum(dist, 0)
    exact = REL_BUCKETS // 2
    big = exact + (jnp.log(jnp.maximum(n, 1).astype(jnp.float32) / exact)
                   / math.log(REL_MAX_DIST / exact) * (REL_BUCKETS - exact)).astype(jnp.int32)
    return jnp.where(n < exact, n, jnp.minimum(big, REL_BUCKETS - 1))


def _bias_kernel(rb_ref, bk_ref, o_ref):
    bk = bk_ref[...]
    for h in range(o_ref.shape[0]):
        acc = jnp.zeros(bk.shape, F32)
        for b in range(REL_BUCKETS):
            acc = jnp.where(bk == b, rb_ref[b, h], acc)
        o_ref[h] = acc


def _bias_table(rel_bias, buckets, rows=256):
    m, w = buckets.shape
    n_heads = rel_bias.shape[1]
    rows = min(rows, m)
    return pl.pallas_call(
        _bias_kernel,
        grid=(m // rows,),
        in_specs=[pl.BlockSpec(memory_space=pltpu.SMEM),
                  pl.BlockSpec((rows, w), lambda i: (i, 0))],
        out_specs=pl.BlockSpec((n_heads, rows, w), lambda i: (0, i, 0)),
        out_shape=jax.ShapeDtypeStruct((n_heads, m, w), F32),
        compiler_params=_params(("arbitrary",)),
        name="nsa_bias_table",
    )(rel_bias, buckets)


def _cmp_kernel(z_ref, pe_ref, w1_ref, w2_ref, o_ref):
    z = z_ref[0, 0]
    r, half = z.shape
    w1 = w1_ref[0]
    top = _dot(z, w1[:half])
    bot = _dot(z, w1[half:])
    bias = _dot(pe_ref[0], w1)[0:1]
    pre = top + pltpu.roll(bot, r - 1, 0) + bias
    o_ref[0, 0] = _dot(_gelu_tanh(pre).astype(BF16), w2_ref[0]).astype(o_ref.dtype)


def _nsa_compress(z, pe, w1, w2):
    bsz, _, r, half = z.shape
    d = w2.shape[-1]
    return pl.pallas_call(
        _cmp_kernel,
        grid=(bsz, 4),
        in_specs=[
            pl.BlockSpec((1, 1, r, half), lambda b, j: (b, j, 0, 0)),
            pl.BlockSpec((1, V7X_SUBLANES, 2 * half), lambda b, j: (j // 2, 0, 0)),
            pl.BlockSpec((1, 2 * half, w1.shape[-1]), lambda b, j: (j // 2, 0, 0)),
            pl.BlockSpec((1, w2.shape[1], d), lambda b, j: (j // 2, 0, 0)),
        ],
        out_specs=pl.BlockSpec((1, 1, r, d), lambda b, j: (b, j, 0, 0)),
        out_shape=jax.ShapeDtypeStruct((bsz, 4, r, d), BF16),
        compiler_params=_params(("arbitrary", "arbitrary")),
        name="nsa_compress",
    )(z, pe, w1, w2)


def _masked_exp(logits, mask):
    logits = jnp.where(mask, logits, NEG)
    m = jnp.max(logits, axis=-1, keepdims=True)
    e = jnp.where(mask, jnp.exp(logits - m), 0.0)
    return e, 1.0 / jnp.maximum(jnp.sum(e, axis=-1, keepdims=True), 1e-30)


def _nsa_kernel(q_ref, sm_ref, kc_ref, vc_ref, bc_ref, bt_ref, ovt_ref,
                ks_ref, vs_ref, kw_ref, vw_ref, o_ref, *, n_cmp, n_top, gate_col):
    g = pl.program_id(1)
    qb = pl.program_id(2)
    qr = NSA_QB
    d = NSA_HEAD_DIM
    hpg = q_ref.shape[2] // d
    s0 = qb * qr
    q = q_ref[0] * (d ** -0.5)
    q4 = jnp.concatenate([q[:, h * d:(h + 1) * d] for h in range(hpg)], axis=0).astype(BF16)
    t_col = s0 + lax.broadcasted_iota(jnp.int32, (qr, 1), 0)
    t4 = jnp.concatenate([t_col] * hpg, axis=0)

    r = kc_ref.shape[2]
    n_sel = ovt_ref.shape[0]
    sc = _dot_nt(q4, kc_ref[0, 0]) + bc_ref[...].reshape(hpg * qr, r)
    n_row = lax.broadcasted_iota(jnp.int32, (1, r), 1)
    cmp_end = n_row * NSA_CMP_STRIDE + (NSA_CMP_BLOCK - 1)
    e4, inv4 = _masked_exp(sc, (cmp_end <= t4) & (n_row < n_cmp))
    p4 = e4 * inv4
    o_cmp = _dot(p4.astype(BF16), vc_ref[0, 0])

    p_sum = p4[0:qr]
    for h in range(1, hpg):
        p_sum = p_sum + p4[h * qr:(h + 1) * qr]
    p_hi = p_sum.astype(BF16)
    p_lo = (p_sum - p_hi.astype(F32)).astype(BF16)
    ovt = ovt_ref[...]
    imp = _dot_nt(ovt, p_hi) + _dot_nt(ovt, p_lo)
    j_col = lax.broadcasted_iota(jnp.int32, (n_sel, 1), 0)
    t_row = s0 + lax.broadcasted_iota(jnp.int32, (1, qr), 1)
    q_blk = t_row // NSA_SEL_BLOCK
    forced = (j_col == 0) | (j_col == q_blk) | (j_col == q_blk - 1)
    score = jnp.where(j_col * NSA_SEL_BLOCK <= t_row,
                      jnp.where(forced, FORCE_SCORE, imp), -FORCE_SCORE)
    rank = jnp.zeros((n_sel, qr), jnp.int32)
    for k in range(n_sel):
        row = score[k:k + 1]
        ahead = (row > score) | ((row == score) & (j_col > k))
        rank = rank + ahead.astype(jnp.int32)
    sel_t = (rank < n_top).astype(BF16)

    tk = NSA_TK
    key_lane = lax.broadcasted_iota(jnp.int32, (1, tk), 1)
    nb = bt_ref.shape[0]

    def sel_step(j, carry):
        m_run, l_run, acc = carry
        k0 = pl.multiple_of(j * tk, tk)
        kt = ks_ref[0, 0, pl.ds(k0, tk), :]
        vt = vs_ref[0, 0, pl.ds(k0, tk), :]
        bias = jnp.concatenate(
            [bt_ref[jnp.clip(qb - (tk // qr) * j - mm, 0, nb - 1)].reshape(hpg * qr, qr)
             for mm in range(tk // qr)], axis=1)
        key = k0 + key_lane
        expand = (j_col == key // NSA_SEL_BLOCK).astype(BF16)
        chosen = _dot_tn(sel_t, expand)
        off = jnp.where((chosen > 0.5) & (key <= t_col), 0.0, NEG)
        s = _dot_nt(q4, kt) + bias + jnp.concatenate([off] * hpg, axis=0)
        m_new = jnp.maximum(m_run, jnp.max(s, axis=-1, keepdims=True))
        scale = jnp.exp(m_run - m_new)
        e = jnp.exp(s - m_new)
        l_new = l_run * scale + jnp.sum(e, axis=-1, keepdims=True)
        acc = acc * scale + _dot(e.astype(BF16), vt)
        return m_new, l_new, acc

    init = (jnp.full((hpg * qr, 1), NEG, F32), jnp.zeros((hpg * qr, 1), F32),
            jnp.zeros((hpg * qr, d), F32))
    _, l_sel, acc_sel = lax.fori_loop(0, (s0 + qr - 1) // tk + 1, sel_step, init)
    o_sel = acc_sel / jnp.maximum(l_sel, 1e-30)

    n_wt = NSA_WINDOW // qr + 1
    kws, vws, biases, masks = [], [], [], []
    i_col = lax.broadcasted_iota(jnp.int32, (qr, 1), 0)
    j_row = lax.broadcasted_iota(jnp.int32, (1, qr), 1)
    for mm in range(n_wt):
        k0 = s0 - NSA_WINDOW + mm * qr
        k0c = pl.multiple_of(jnp.maximum(k0, 0), qr)
        kws.append(kw_ref[0, 0, pl.ds(k0c, qr), :])
        vws.append(vw_ref[0, 0, pl.ds(k0c, qr), :])
        biases.append(bt_ref[n_wt - 1 - mm].reshape(hpg * qr, qr))
        dist = (n_wt - 1 - mm) * qr + i_col - j_row
        masks.append((dist >= 0) & (dist < NSA_WINDOW) & (k0 >= 0))
    sw = _dot_nt(q4, jnp.concatenate(kws, axis=0)) + jnp.concatenate(biases, axis=1)
    mw = jnp.concatenate(masks, axis=1)
    ew, inv_w = _masked_exp(sw, jnp.concatenate([mw] * hpg, axis=0))
    o_win = _dot(ew.astype(BF16), jnp.concatenate(vws, axis=0)) * inv_w

    sm = sm_ref[0]
    lane = lax.broadcasted_iota(jnp.int32, sm.shape, 1)

    def gate(h, branch):
        col = gate_col + 3 * (g * hpg + h) + branch
        return _sigmoid(jnp.sum(jnp.where(lane == col, sm, 0.0), axis=1, keepdims=True))

    outs = []
    for h in range(hpg):
        sl = slice(h * qr, (h + 1) * qr)
        outs.append(gate(h, 0) * o_cmp[sl] + gate(h, 1) * o_sel[sl] + gate(h, 2) * o_win[sl])
    o_ref[0] = jnp.concatenate(outs, axis=1).astype(o_ref.dtype)


def _nsa_attention(h3, kv_t, cmp_kv, bias_c, bias_t, ovt, *, q_col, small_col, gate_col, n_cmp):
    bsz, t_len, _ = h3.shape
    g_n = NSA_KV_GROUPS
    d = NSA_HEAD_DIM
    n_heads = bias_c.shape[0]
    hpg = n_heads // g_n
    qw = hpg * d
    r = cmp_kv.shape[2]
    n_sel = t_len // NSA_SEL_BLOCK
    nb = bias_t.shape[0]
    kv_spec = lambda part: pl.BlockSpec((1, 1, t_len, d), lambda b, g, q: (b, 2 * part + g, 0, 0))
    return pl.pallas_call(
        functools.partial(_nsa_kernel, n_cmp=n_cmp, n_top=min(NSA_TOP_BLOCKS, n_sel),
                          gate_col=gate_col),
        grid=(bsz, g_n, t_len // NSA_QB),
        in_specs=[
            pl.BlockSpec((1, NSA_QB, qw), lambda b, g, q: (b, q, q_col // qw + g)),
            pl.BlockSpec((1, NSA_QB, 256), lambda b, g, q: (b, q, small_col // 256)),
            pl.BlockSpec((1, 1, r, d), lambda b, g, q: (b, g, 0, 0)),
            pl.BlockSpec((1, 1, r, d), lambda b, g, q: (b, g_n + g, 0, 0)),
            pl.BlockSpec((hpg, NSA_QB, r), lambda b, g, q: (g, q, 0)),
            pl.BlockSpec((nb, hpg, NSA_QB, NSA_QB), lambda b, g, q: (0, g, 0, 0)),
            pl.BlockSpec((n_sel, r), lambda b, g, q: (0, 0)),
            kv_spec(2), kv_spec(3), kv_spec(4), kv_spec(5),
        ],
        out_specs=pl.BlockSpec((1, NSA_QB, qw), lambda b, g, q: (b, q, g)),
        out_shape=jax.ShapeDtypeStruct((bsz, t_len, n_heads * d), BF16),
        compiler_params=_params(("arbitrary", "arbitrary", "arbitrary")),
        name="nsa_attention",
    )(h3, h3, cmp_kv, cmp_kv, bias_c, bias_t, ovt, kv_t, kv_t, kv_t, kv_t)


def _outproj_kernel(x_ref, ya_ref, yb_ref, yc_ref, wa_ref, wb_ref, wc_ref, g_ref, b_ref, o_ref,
                    *, alpha):
    mix = _dot(ya_ref[...], wa_ref[...]) + _dot(yb_ref[...], wb_ref[...]) + _dot(yc_ref[...], wc_ref[...])
    o_ref[...] = _layer_norm(alpha * x_ref[...] + mix, g_ref[...], b_ref[...])


def _outproj_ln(x2d, ya, yb, yc, w_out, g, b, alpha, tm=256):
    n, d = x2d.shape
    wa_n, wb_n, wc_n = ya.shape[1], yb.shape[1], yc.shape[1]
    w16 = w_out.astype(BF16)
    row = lambda w: pl.BlockSpec((tm, w), lambda i: (i, 0))
    full = lambda r: pl.BlockSpec((r, d), lambda i: (0, 0))
    return pl.pallas_call(
        functools.partial(_outproj_kernel, alpha=alpha),
        grid=(n // tm,),
        in_specs=[row(d), row(wa_n), row(wb_n), row(wc_n),
                  full(wa_n), full(wb_n), full(wc_n), full(1), full(1)],
        out_specs=row(d),
        out_shape=jax.ShapeDtypeStruct((n, d), F32),
        compiler_params=_params(("arbitrary",)),
        name="outproj_ln",
    )(x2d, ya, yb, yc, w16[:wa_n], w16[wa_n:wa_n + wb_n], w16[wa_n + wb_n:],
      g.reshape(1, d), b.reshape(1, d))


def _mm_kernel(x_ref, w_ref, o_ref):
    o_ref[...] = _dot(x_ref[...].astype(BF16), w_ref[...]).astype(o_ref.dtype)


def _matmul(x2d, w16, out_dtype, tm=256):
    n, d = x2d.shape
    cols = w16.shape[1]
    tm = min(tm, n)
    return pl.pallas_call(
        _mm_kernel,
        grid=(n // tm,),
        in_specs=[pl.BlockSpec((tm, d), lambda i: (i, 0)),
                  pl.BlockSpec((d, cols), lambda i: (0, 0))],
        out_specs=pl.BlockSpec((tm, cols), lambda i: (i, 0)),
        out_shape=jax.ShapeDtypeStruct((n, cols), out_dtype),
        compiler_params=_params(("arbitrary",)),
        name="mem_kv_proj",
    )(x2d, w16)


def _xattn_kernel(x_ref, kv_ref, wq_ref, wo_ref, g_ref, b_ref, o_ref, o16_ref, oslab_ref, *, alpha):
    x = x_ref[...]
    q = _dot(x.astype(BF16), wq_ref[...])
    kv = kv_ref[0]
    width = wq_ref.shape[1]
    hd = XA_HEAD_DIM
    outs = []
    for h in range(width // hd):
        qh = q[:, h * hd:(h + 1) * hd].astype(BF16)
        kh = kv[:, h * hd:(h + 1) * hd]
        vh = kv[:, width + h * hd:width + (h + 1) * hd]
        logits = _dot_nt(qh, kh) * (hd ** -0.5)
        m = jnp.max(logits, axis=-1, keepdims=True)
        e = jnp.exp(logits - m)
        p = e / jnp.sum(e, axis=-1, keepdims=True)
        outs.append(_dot(p.astype(BF16), vh))
    o = jnp.concatenate(outs, axis=1).astype(BF16)
    y = _layer_norm(alpha * x + _dot(o, wo_ref[...]), g_ref[...], b_ref[...])
    o_ref[...] = y
    o16_ref[...] = y.astype(BF16)
    tm = y.shape[0]
    cpr = y.shape[1] // V7X_LANES
    for c in range(cpr):
        oslab_ref[pl.ds(c, tm, stride=cpr), :] = y[:, c * V7X_LANES:(c + 1) * V7X_LANES]


def _cross_attention_ln(x2d, kv, wq, wo, g, b, alpha, t_len, tm=256):
    n, d = x2d.shape
    width = wq.shape[1]
    m_len = kv.shape[1]
    per_b = t_len // tm
    cpr = d // V7X_LANES
    return pl.pallas_call(
        functools.partial(_xattn_kernel, alpha=alpha),
        grid=(n // tm,),
        in_specs=[
            pl.BlockSpec((tm, d), lambda i: (i, 0)),
            pl.BlockSpec((1, m_len, 2 * width), lambda i: (i // per_b, 0, 0)),
            pl.BlockSpec((d, width), lambda i: (0, 0)),
            pl.BlockSpec((width, d), lambda i: (0, 0)),
            pl.BlockSpec((1, d), lambda i: (0, 0)),
            pl.BlockSpec((1, d), lambda i: (0, 0)),
        ],
        out_specs=[pl.BlockSpec((tm, d), lambda i: (i, 0)), pl.BlockSpec((tm, d), lambda i: (i, 0)),
                   pl.BlockSpec((tm * cpr, V7X_LANES), lambda i: (i, 0))],
        out_shape=[jax.ShapeDtypeStruct((n, d), F32), jax.ShapeDtypeStruct((n, d), BF16),
                   jax.ShapeDtypeStruct((n * cpr, V7X_LANES), F32)],
        compiler_params=_params(("arbitrary",)),
        name="cross_attention_ln",
    )(x2d, kv, wq.astype(BF16), wo.astype(BF16), g.reshape(1, d), b.reshape(1, d))


def _rank_rows(v, idx_col):
    rank = jnp.zeros(v.shape, jnp.int32)
    for k in range(v.shape[0]):
        row = v[k:k + 1]
        rank = rank + ((row > v) | ((row == v) & (idx_col > k))).astype(jnp.int32)
    return rank


def _router_kernel(x_ref, w_ref, b_ref, e_ref, g_ref, cnt_ref):
    n_exp = w_ref.shape[0]
    gsz = n_exp // N_GROUPS
    logits = lax.dot_general(w_ref[...], x_ref[...], (((1,), (1,)), ((), ())),
                             preferred_element_type=F32, precision=lax.Precision.HIGHEST)
    scores = _sigmoid(logits)
    choice = scores + b_ref[...]
    tm = scores.shape[1]
    sub = lax.broadcasted_iota(jnp.int32, (gsz, 1), 0)
    grp_rows = []
    for gi in range(N_GROUPS):
        slab = choice[gi * gsz:(gi + 1) * gsz]
        m1 = jnp.max(slab, axis=0, keepdims=True)
        first = jnp.min(jnp.where(slab == m1, sub, gsz), axis=0, keepdims=True)
        m2 = jnp.max(jnp.where(sub == first, NEG, slab), axis=0, keepdims=True)
        grp_rows.append(m1 + m2)
    grp = jnp.concatenate(grp_rows, axis=0)
    g_idx = lax.broadcasted_iota(jnp.int32, (N_GROUPS, 1), 0)
    grp_ok = _rank_rows(grp, g_idx) < TOPK_GROUPS
    ok = jnp.concatenate([jnp.broadcast_to(grp_ok[gi:gi + 1], (gsz, tm)) for gi in range(N_GROUPS)], axis=0)
    e_idx = lax.broadcasted_iota(jnp.int32, (n_exp, 1), 0)
    rank = _rank_rows(jnp.where(ok, choice, -1e9), e_idx)
    denom = jnp.sum(jnp.where(rank < TOP_K, scores, 0.0), axis=0, keepdims=True)
    e_rows, g_rows = [], []
    for k in range(TOP_K):
        hit = rank == k
        e_rows.append(jnp.sum(jnp.where(hit, e_idx, 0), axis=0, keepdims=True))
        g_rows.append(jnp.sum(jnp.where(hit, scores, 0.0), axis=0, keepdims=True) / denom * ROUTED_SCALE)
    e_ref[...] = jnp.concatenate(e_rows, axis=0)
    g_ref[...] = jnp.concatenate(g_rows, axis=0)

    @pl.when(pl.program_id(0) == 0)
    def _():
        cnt_ref[...] = jnp.zeros_like(cnt_ref)

    cnt_ref[...] += jnp.sum((rank < TOP_K).astype(jnp.int32), axis=1, keepdims=True)


def _router(x2d, w_router, bias, tm=512):
    n, d = x2d.shape
    n_exp = w_router.shape[1]
    return pl.pallas_call(
        _router_kernel,
        grid=(n // tm,),
        in_specs=[pl.BlockSpec((tm, d), lambda i: (i, 0)),
                  pl.BlockSpec((n_exp, d), lambda i: (0, 0)),
                  pl.BlockSpec((n_exp, 1), lambda i: (0, 0))],
        out_specs=[pl.BlockSpec((TOP_K, tm), lambda i: (0, i)), pl.BlockSpec((TOP_K, tm), lambda i: (0, i)),
                   pl.BlockSpec((n_exp, 1), lambda i: (0, 0))],
        out_shape=[jax.ShapeDtypeStruct((TOP_K, n), jnp.int32), jax.ShapeDtypeStruct((TOP_K, n), F32),
                   jax.ShapeDtypeStruct((n_exp, 1), jnp.int32)],
        compiler_params=_params(("arbitrary",)),
        name="moe_router",
    )(x2d, w_router.T, bias.reshape(n_exp, 1))


def _expert_kernel(e_ref, b_ref, lo_ref, hi_ref, first_ref,
                   tok_ref, dst_ref, tok_next_ref, x_hbm, wt_ref, wg_ref, wu_ref, wd_ref, y_hbm,
                   xbuf, obuf, wg16, wu16, wd16, gsem, ssem):
    i = pl.program_id(0)
    n_items = pl.num_programs(0)
    slot = i % 2
    rows = wt_ref.shape[0]
    cpr = xbuf.shape[1] // rows
    lo, hi = lo_ref[i], hi_ref[i]

    def slab(ref, r):
        return ref.at[pl.ds(pl.multiple_of(r * cpr, cpr), cpr)]

    def for_rows(r0, r1, start_row):
        n_groups = (r1 - r0) // MOE_DMA_UNROLL

        def group(g, _):
            base = r0 + g * MOE_DMA_UNROLL
            for u in range(MOE_DMA_UNROLL):
                start_row(base + u, u % 2)
            return 0

        def single(r, _):
            start_row(r, 0)
            return 0

        lax.fori_loop(0, n_groups, group, 0)
        lax.fori_loop(r0 + n_groups * MOE_DMA_UNROLL, r1, single, 0)

    def start_gather(idx_ref, r0, r1, s):
        def start_row(r, priority):
            pltpu.make_async_copy(slab(x_hbm, idx_ref[0, 0, r]), slab(xbuf.at[s], r),
                                  gsem.at[s]).start(priority=priority)

        for_rows(r0, r1, start_row)

    def wait_gather(n, s):
        @pl.when(n > 0)
        def _():
            pltpu.make_async_copy(x_hbm.at[pl.ds(0, n * cpr)], xbuf.at[s, pl.ds(0, n * cpr)],
                                  gsem.at[s]).wait()

    def wait_scatter(n, s):
        @pl.when(n > 0)
        def _():
            pltpu.make_async_copy(obuf.at[s, pl.ds(0, n * cpr)], y_hbm.at[pl.ds(0, n * cpr)],
                                  ssem.at[s]).wait()

    @pl.when(i == 0)
    def _():
        xbuf[...] = jnp.zeros_like(xbuf)
        start_gather(tok_ref, lo, hi, 0)

    @pl.when(i + 1 < n_items)
    def _():
        start_gather(tok_next_ref, lo_ref[i + 1], hi_ref[i + 1], 1 - slot)

    @pl.when(first_ref[i] == 1)
    def _():
        wg16[...] = wg_ref[...].astype(BF16)
        wu16[...] = wu_ref[...].astype(BF16)
        wd16[...] = wd_ref[...].astype(BF16)

    wait_gather(hi - lo, slot)

    @pl.when(i >= 2)
    def _():
        wait_scatter(hi_ref[i - 2] - lo_ref[i - 2], slot)

    @pl.when(hi > lo)
    def _():
        xb = jnp.concatenate([xbuf[slot, pl.ds(c, rows, stride=cpr), :] for c in range(cpr)], axis=1)
        xb = xb.astype(BF16)
        hid = _silu(_dot(xb, wg16[...])) * _dot(xb, wu16[...])
        out = _dot(hid.astype(BF16), wd16[...]) * wt_ref[...]
        for c in range(cpr):
            obuf[slot, pl.ds(c, rows, stride=cpr), :] = out[:, c * V7X_LANES:(c + 1) * V7X_LANES]

        def start_row(r, priority):
            pltpu.make_async_copy(slab(obuf.at[slot], r), slab(y_hbm, dst_ref[0, 0, r]),
                                  ssem.at[slot]).start(priority=priority)

        for_rows(lo, hi, start_row)

    @pl.when(i == n_items - 1)
    def _():
        @pl.when(i >= 1)
        def _():
            wait_scatter(hi_ref[i - 1] - lo_ref[i - 1], 1 - slot)

        wait_scatter(hi - lo, slot)


def _experts(x_slab, plan, w_gate, w_up, w_down, layer):
    item_e, item_b, item_lo, item_hi, item_first, row_tok, row_dst, row_w = plan
    d = w_gate.shape[2]
    n = x_slab.shape[0] * V7X_LANES // d
    n_items = item_e.shape[0]
    dff = w_gate.shape[3]
    rows = MOE_ROWS
    n_blk = row_tok.shape[0] // rows
    cpr = d // V7X_LANES
    wspec = lambda r, c: pl.BlockSpec((None, None, r, c), lambda i, e, *_: (layer, e[i], 0, 0))
    cur = lambda i, e, b, *_: (b[i], 0, 0)
    nxt = lambda i, e, b, *_: (b[jnp.minimum(i + 1, n_items - 1)], 0, 0)
    grid_spec = pltpu.PrefetchScalarGridSpec(
        num_scalar_prefetch=5,
        grid=(n_items,),
        in_specs=[
            pl.BlockSpec((1, 1, rows), cur, memory_space=pltpu.SMEM),
            pl.BlockSpec((1, 1, rows), cur, memory_space=pltpu.SMEM),
            pl.BlockSpec((1, 1, rows), nxt, memory_space=pltpu.SMEM),
            pl.BlockSpec(memory_space=pl.ANY),
            pl.BlockSpec((rows, 1), lambda i, e, b, *_: (b[i], 0)),
            wspec(d, dff), wspec(d, dff), wspec(dff, d),
        ],
        out_specs=pl.BlockSpec(memory_space=pl.ANY),
        scratch_shapes=[
            pltpu.VMEM((2, rows * cpr, V7X_LANES), F32), pltpu.VMEM((2, rows * cpr, V7X_LANES), F32),
            pltpu.VMEM((d, dff), BF16), pltpu.VMEM((d, dff), BF16), pltpu.VMEM((dff, d), BF16),
            pltpu.SemaphoreType.DMA((2,)), pltpu.SemaphoreType.DMA((2,)),
        ],
    )
    tok3 = row_tok.reshape(n_blk, 1, rows)
    return pl.pallas_call(
        _expert_kernel,
        grid_spec=grid_spec,
        out_shape=jax.ShapeDtypeStruct((n * TOP_K * cpr, V7X_LANES), F32),
        compiler_params=_params(("arbitrary",)),
        name="moe_experts",
    )(item_e, item_b, item_lo, item_hi, item_first,
      tok3, row_dst.reshape(n_blk, 1, rows), tok3,
      x_slab, row_w.reshape(n_blk * rows, 1), w_gate, w_up, w_down)


def _combine_kernel(x_ref, x16_ref, y_ref, wg_ref, wu_ref, wd_ref, g_ref, b_ref, o_ref, acc_ref,
                    *, alpha):
    x16 = x16_ref[...]
    shared = _dot((_silu(_dot(x16, wg_ref[...])) * _dot(x16, wu_ref[...])).astype(BF16), wd_ref[...])
    tm, d = x_ref.shape
    cpr = d // V7X_LANES
    tot = y_ref[0]
    for k in range(1, TOP_K):
        tot = tot + y_ref[k]
    acc_ref[...] = tot
    routed = jnp.concatenate([acc_ref[pl.ds(c, tm, stride=cpr), :] for c in range(cpr)], axis=1)
    o_ref[...] = _layer_norm(alpha * x_ref[...] + (routed + shared), g_ref[...], b_ref[...])


def _moe_combine_ln(x2d, x16, y8, ws_gate, ws_up, ws_down, g, b, alpha, tm=128):
    n, d = x2d.shape
    dff = ws_gate.shape[1]
    cpr = d // V7X_LANES
    row = lambda w: pl.BlockSpec((tm, w), lambda i: (i, 0))
    full = lambda r, c: pl.BlockSpec((r, c), lambda i: (0, 0))
    return pl.pallas_call(
        functools.partial(_combine_kernel, alpha=alpha),
        grid=(n // tm,),
        in_specs=[row(d), row(d), pl.BlockSpec((TOP_K, tm * cpr, V7X_LANES), lambda i: (0, i, 0)),
                  full(d, dff), full(d, dff), full(dff, d), full(1, d), full(1, d)],
        out_specs=row(d),
        out_shape=jax.ShapeDtypeStruct((n, d), F32),
        scratch_shapes=[pltpu.VMEM((tm * cpr, V7X_LANES), F32)],
        compiler_params=_params(("arbitrary",)),
        name="moe_combine_ln",
    )(x2d, x16, y8.reshape(TOP_K, n * cpr, V7X_LANES), ws_gate.astype(BF16), ws_up.astype(BF16),
      ws_down.astype(BF16), g.reshape(1, d), b.reshape(1, d))


def _dispatch_plan(top_e, gate, counts):
    n = top_e.shape[1]
    n_exp = counts.shape[0]
    rows = MOE_ROWS
    n_assign = n * TOP_K
    n_items = n_assign // rows + n_exp
    a_iota = jnp.arange(n_assign, dtype=jnp.int32)
    _, row_dst, row_w = lax.sort((top_e.reshape(-1), a_iota, gate.reshape(-1)), num_keys=1)
    row_tok = row_dst % n
    ends = jnp.cumsum(counts)
    starts = ends - counts
    n_blk_e = jnp.where(counts > 0, (ends - 1) // rows - starts // rows + 1, 0)
    item_end = jnp.cumsum(n_blk_e)
    item_start = item_end - n_blk_e
    t = jnp.arange(n_items, dtype=jnp.int32)
    active = t < item_end[-1]
    item_e = jnp.minimum(jnp.sum((item_end[None, :] <= t[:, None]).astype(jnp.int32), axis=1), n_exp - 1)
    onehot = item_e[:, None] == jnp.arange(n_exp, dtype=jnp.int32)[None, :]
    pick = lambda v: jnp.sum(jnp.where(onehot, v[None, :], 0), axis=1)
    s_e, e_e, i_e = pick(starts), pick(ends), pick(item_start)
    item_b = jnp.clip(s_e // rows + (t - i_e), 0, n_assign // rows - 1)
    item_lo = jnp.where(active, jnp.clip(s_e - item_b * rows, 0, rows), 0)
    item_hi = jnp.where(active, jnp.clip(e_e - item_b * rows, 0, rows), 0)
    item_first = (active & (t == i_e)).astype(jnp.int32)
    i32 = lambda v: v.astype(jnp.int32)
    return (i32(item_e), i32(item_b), i32(item_lo), i32(item_hi), item_first,
            i32(row_tok), i32(row_dst), row_w)


def kernel(x, mem, rel_bias, w_in, rg_conv_w, rg_conv_b, rg_wa, rg_ba, rg_wi, rg_bi, rg_lambda,
           dn_conv_w, dn_a_log, dn_dt_bias, dn_norm_w, nsa_pe_k, nsa_pe_v, nsa_phi_k1, nsa_phi_k2,
           nsa_phi_v1, nsa_phi_v2, w_out, ln1_g, ln1_b, xa_wq, xa_wk, xa_wv, xa_wo, ln2_g, ln2_b,
           moe_router, moe_router_bias, moe_w_gate, moe_w_up, moe_w_down, shared_w_gate, shared_w_up,
           shared_w_down, ln3_g, ln3_b):
    bsz, t_len, d_model = x.shape
    depth = w_in.shape[0]
    n_tok = bsz * t_len
    alpha = (2 * depth) ** 0.25
    rg_w = rg_conv_w.shape[2]
    dn_w = dn_conv_w.shape[2] // 3
    dn_heads = dn_a_log.shape[1]
    nsa_heads = rel_bias.shape[1]
    nsa_w = nsa_heads * NSA_HEAD_DIM
    kv_w = NSA_KV_GROUPS * NSA_HEAD_DIM
    n_exp = moe_router.shape[2]

    sizes = (rg_w, rg_w, dn_w, dn_w, dn_w, dn_w, dn_heads, dn_heads, nsa_w, 6 * kv_w, 3 * nsa_heads)
    offs = [0]
    for s in sizes:
        offs.append(offs[-1] + s)
    col = lambda w, i: w[:, :, offs[i]:offs[i + 1]]
    main = offs[6]
    q_col = main
    small_col = q_col + nsa_w
    kv_col = small_col + 256
    gate_col = 2 * dn_heads
    small = jnp.concatenate([col(w_in, 6), col(w_in, 7), col(w_in, 10)], axis=2)
    small = jnp.pad(small, ((0, 0), (0, 0), (0, 256 - small.shape[2])))
    w_slab = jnp.concatenate([w_in[:, :, :main], col(w_in, 8), small, col(w_in, 9)], axis=2).astype(BF16)
    w_small_t = jnp.swapaxes(jnp.concatenate([col(w_in, 6), col(w_in, 7)], axis=2), 1, 2).astype(BF16)

    r_cmp = t_len // NSA_CMP_STRIDE
    n_cmp = (t_len - NSA_CMP_BLOCK) // NSA_CMP_STRIDE + 1
    n_sel = t_len // NSA_SEL_BLOCK
    cmp_start = jnp.arange(r_cmp) * NSA_CMP_STRIDE
    cmp_end = cmp_start + NSA_CMP_BLOCK - 1
    sel_start = jnp.arange(n_sel) * NSA_SEL_BLOCK
    ovt = ((cmp_start[None, :] < sel_start[:, None] + NSA_SEL_BLOCK)
           & (cmp_end[None, :] >= sel_start[:, None])
           & (jnp.arange(r_cmp)[None, :] < n_cmp)).astype(BF16)
    bucket_c = _rel_bucket(jnp.arange(t_len)[:, None] - cmp_end[None, :]).astype(jnp.int32)
    tile = jnp.arange(NSA_BIAS_TILES * NSA_QB)[:, None] - jnp.arange(NSA_QB)[None, :]
    bucket_t = _rel_bucket(tile).astype(jnp.int32)
    bias_c = _bias_table(rel_bias, bucket_c)
    bias_t = _bias_table(rel_bias, bucket_t).reshape(nsa_heads, NSA_BIAS_TILES, NSA_QB, NSA_QB)
    bias_t = jnp.swapaxes(bias_t, 0, 1)

    cmp_in = NSA_CMP_BLOCK * NSA_HEAD_DIM
    x2d = x.reshape(n_tok, d_model)
    mem2d = mem.reshape(-1, d_model)
    for l in range(depth):
        h, small_t = _inproj(x2d, w_slab, w_small_t, l)
        h3 = h.reshape(bsz, t_len, -1)
        y_a = _rglru(h3, rg_conv_w[l], rg_conv_b[l], rg_wa[l], rg_wi[l], rg_ba[l], rg_bi[l],
                     rg_lambda[l], rg_w)
        y_b = _deltanet(h3, small_t, dn_conv_w[l], dn_a_log[l], dn_dt_bias[l], dn_norm_w[l],
                        n_heads=dn_heads, col0=2 * rg_w, small_col=small_col)
        kv_t = h3[:, :, kv_col:].reshape(bsz, t_len, 6 * NSA_KV_GROUPS, NSA_HEAD_DIM)
        kv_t = jnp.swapaxes(kv_t, 1, 2).astype(BF16)
        z = kv_t[:, :2 * NSA_KV_GROUPS].reshape(bsz, 2 * NSA_KV_GROUPS, r_cmp, cmp_in // 2)
        pe = jnp.stack([nsa_pe_k[l], nsa_pe_v[l]]).reshape(2, 1, cmp_in)
        pe = jnp.broadcast_to(pe, (2, V7X_SUBLANES, cmp_in)).astype(BF16)
        w1 = jnp.stack([nsa_phi_k1[l], nsa_phi_v1[l]]).astype(BF16)
        w2 = jnp.stack([nsa_phi_k2[l], nsa_phi_v2[l]]).astype(BF16)
        cmp_kv = _nsa_compress(z, pe, w1, w2)
        y_c = _nsa_attention(h3, kv_t, cmp_kv, bias_c, bias_t, ovt, q_col=q_col,
                             small_col=small_col, gate_col=gate_col, n_cmp=n_cmp)
        x2d = _outproj_ln(x2d, y_a.reshape(n_tok, -1), y_b.reshape(n_tok, -1), y_c.reshape(n_tok, -1),
                          w_out[l], ln1_g[l], ln1_b[l], alpha)
        w_kv = jnp.concatenate([xa_wk[l], xa_wv[l]], axis=1).astype(BF16)
        kv_mem = _matmul(mem2d, w_kv, BF16).reshape(bsz, mem.shape[1], -1)
        x2d, x16, x_slab = _cross_attention_ln(x2d, kv_mem, xa_wq[l], xa_wo[l], ln2_g[l], ln2_b[l],
                                               alpha, t_len)
        top_e, gate, counts = _router(x2d, moe_router[l], moe_router_bias[l])
        plan = _dispatch_plan(top_e, gate, counts.reshape(-1))
        y8 = _experts(x_slab, plan, moe_w_gate, moe_w_up, moe_w_down, l)
        x2d = _moe_combine_ln(x2d, x16, y8, shared_w_gate[l], shared_w_up[l], shared_w_down[l],
                              ln3_g[l], ln3_b[l], alpha)
    return x2d.reshape(bsz, t_len, d_model)
```

```python
import functools
import math

import jax
import jax.numpy as jnp
from jax import lax
from jax.experimental import pallas as pl
from jax.experimental.pallas import tpu as pltpu

F32 = jnp.float32
BF16 = jnp.bfloat16

LN_EPS = 1e-5
NEG = -1e30

RG_BLOCKS = 4
RG_CONV = 4
RG_C = 8.0
DN_HEAD_DIM = 128
DN_CONV = 4
DN_CHUNK = 64
DN_ROWS = 256
NSA_HEAD_DIM = 64
NSA_KV_GROUPS = 2
NSA_CMP_BLOCK = 32
NSA_CMP_STRIDE = 16
NSA_SEL_BLOCK = 64
NSA_TOP_BLOCKS = 16
NSA_WINDOW = 512
FORCE_SCORE = 1e4
REL_BUCKETS = 32
REL_MAX_DIST = 1024
XA_HEADS = 4
XA_HEAD_DIM = 128
TOP_K = 8
N_GROUPS = 8
TOPK_GROUPS = 4
ROUTED_SCALE = 2.5
MOE_ROWS = 256
MOE_DMA_UNROLL = 8

V7X_LANES = 128
V7X_SUBLANES = 8
V7X_VMEM_LIMIT = 56 * 1024 * 1024


def _params(semantics, vmem=V7X_VMEM_LIMIT):
    return pltpu.CompilerParams(dimension_semantics=semantics, vmem_limit_bytes=vmem)


def _sigmoid(x):
    return 0.5 * (1.0 + jnp.tanh(0.5 * x))


def _softplus(x):
    return jnp.maximum(x, 0.0) + jnp.log1p(jnp.exp(-jnp.abs(x)))


def _silu(x):
    return x * _sigmoid(x)


def _gelu_tanh(x):
    c = math.sqrt(2.0 / math.pi)
    return x * (0.5 * (1.0 + jnp.tanh(c * (x + 0.044715 * (x * x * x)))))


def _layer_norm(v, g, b):
    mu = jnp.mean(v, axis=-1, keepdims=True)
    vc = v - mu
    var = jnp.mean(vc * vc, axis=-1, keepdims=True)
    return vc * lax.rsqrt(var + LN_EPS) * g + b


def _dot(a, b):
    return jnp.dot(a, b, preferred_element_type=F32)


def _dot_nt(a, b):
    return lax.dot_general(a, b, (((1,), (1,)), ((), ())), preferred_element_type=F32)


def _dot_tn(a, b):
    return lax.dot_general(a, b, (((0,), (0,)), ((), ())), preferred_element_type=F32)


def _inproj_kernel(x_ref, w_ref, wt_ref, o_ref, ot_ref, xb_ref):
    @pl.when(pl.program_id(1) == 0)
    def _():
        xb_ref[...] = x_ref[...].astype(BF16)
        ot_ref[...] = _dot_nt(wt_ref[...], xb_ref[...])

    o_ref[...] = _dot(xb_ref[...], w_ref[...])


def _inproj(x2d, w, wt, layer, tm=1024, tn=512):
    n, d = x2d.shape
    cols = w.shape[2]
    return pl.pallas_call(
        _inproj_kernel,
        grid=(n // tm, cols // tn),
        in_specs=[
            pl.BlockSpec((tm, d), lambda i, j: (i, 0)),
            pl.BlockSpec((None, d, tn), lambda i, j: (layer, 0, j)),
            pl.BlockSpec((None, wt.shape[1], d), lambda i, j: (layer, 0, 0)),
        ],
        out_specs=[
            pl.BlockSpec((tm, tn), lambda i, j: (i, j)),
            pl.BlockSpec((wt.shape[1], tm), lambda i, j: (0, i)),
        ],
        out_shape=[
            jax.ShapeDtypeStruct((n, cols), F32),
            jax.ShapeDtypeStruct((wt.shape[1], n), F32),
        ],
        scratch_shapes=[pltpu.VMEM((tm, d), BF16)],
        compiler_params=_params(("arbitrary", "arbitrary")),
        name="inproj",
    )(x2d, w, wt)


def _shifted_rows(halo, x, s):
    cat = jnp.concatenate([halo, x], axis=0)
    return pltpu.roll(cat, s, 0)[V7X_SUBLANES:]


def _causal_conv(halo, x, w):
    k = w.shape[0]
    y = x * w[k - 1:k]
    for s in range(1, k):
        y = y + _shifted_rows(halo, x, s) * w[k - 1 - s:k - s]
    return y


def _rg_kernel(xh_ref, x_ref, g_ref, cw_ref, cb_ref, wa_ref, wi_ref, ba_ref, bi_ref, lam_ref,
               o_ref, h_ref):
    c = pl.program_id(1)
    x = x_ref[0]
    rows, width = x.shape
    blk = width // RG_BLOCKS
    halo = jnp.where(c > 0, xh_ref[0], 0.0)
    xc = _causal_conv(halo, x, cw_ref[...]) + cb_ref[...]
    xcb = xc.astype(BF16)
    r_parts, i_parts = [], []
    for n in range(RG_BLOCKS):
        xg = xcb[:, n * blk:(n + 1) * blk]
        r_parts.append(_dot(xg, wa_ref[n]))
        i_parts.append(_dot(xg, wi_ref[n]))
    r = _sigmoid(jnp.concatenate(r_parts, axis=1) + ba_ref[...])
    gi = _sigmoid(jnp.concatenate(i_parts, axis=1) + bi_ref[...])
    log_a = -RG_C * r * _softplus(-lam_ref[...])
    a = jnp.exp(log_a)
    th = jnp.tanh(log_a)
    b = jnp.sqrt(-2.0 * th / (1.0 - th)) * (gi * xc)
    row = lax.broadcasted_iota(jnp.int32, (rows, 1), 0)
    s = 1
    while s < rows:
        a_s = pltpu.roll(a, s, 0)
        b_s = pltpu.roll(b, s, 0)
        m = row >= s
        b = jnp.where(m, a * b_s + b, b)
        a = jnp.where(m, a * a_s, a)
        s *= 2

    @pl.when(c == 0)
    def _():
        h_ref[...] = jnp.zeros_like(h_ref)

    h = a * h_ref[0:1] + b
    h_ref[...] = jnp.broadcast_to(h[rows - 1:rows], h_ref.shape)
    o_ref[0] = (h * _gelu_tanh(g_ref[0])).astype(o_ref.dtype)


def _rglru(h3, cw, cb, wa, wi, ba, bi, lam, width, rows=512):
    bsz, t_len, _ = h3.shape
    rows = min(rows, t_len)
    hb = rows // V7X_SUBLANES
    vec = lambda v: v.reshape(1, width)
    full = lambda shape: pl.BlockSpec(shape, lambda b, c: (0,) * len(shape))
    return pl.pallas_call(
        _rg_kernel,
        grid=(bsz, t_len // rows),
        in_specs=[
            pl.BlockSpec((1, V7X_SUBLANES, width), lambda b, c: (b, jnp.maximum(c * hb - 1, 0), 0)),
            pl.BlockSpec((1, rows, width), lambda b, c: (b, c, 0)),
            pl.BlockSpec((1, rows, width), lambda b, c: (b, c, 1)),
            full((RG_CONV, width)), full((1, width)),
            full(wa.shape), full(wi.shape),
            full((1, width)), full((1, width)), full((1, width)),
        ],
        out_specs=pl.BlockSpec((1, rows, width), lambda b, c: (b, c, 0)),
        out_shape=jax.ShapeDtypeStruct((bsz, t_len, width), BF16),
        scratch_shapes=[pltpu.VMEM((V7X_SUBLANES, width), F32)],
        compiler_params=_params(("arbitrary", "arbitrary")),
        name="rglru",
    )(h3, h3, h3, cw, vec(cb), wa.astype(BF16), wi.astype(BF16), vec(ba), vec(bi), vec(lam))


def _segment_cumsum(v, axis, seg):
    pos = lax.broadcasted_iota(jnp.int32, v.shape, axis) % seg
    s = 1
    while s < seg:
        v = v + jnp.where(pos >= s, pltpu.roll(v, s, axis), 0.0)
        s *= 2
    return v


def _l2_normalize(v):
    return v * lax.rsqrt(jnp.sum(v * v, axis=-1, keepdims=True) + 1e-6)


def _dn_kernel(alog_ref, dtb_ref,
               qh_ref, q_ref, kh_ref, k_ref, vh_ref, v_ref, z_ref, sm_ref, smt_ref,
               cwq_ref, cwk_ref, cwv_ref, nw_ref, o_ref, state_ref, *, n_heads):
    c = pl.program_id(2)
    rows = q_ref.shape[1]
    dk = DN_HEAD_DIM
    heads_here = q_ref.shape[2] // dk
    ch = DN_CHUNK

    @pl.when(c == 0)
    def _():
        state_ref[...] = jnp.zeros_like(state_ref)

    def conv_silu(h_ref, x_ref, w_ref):
        halo = jnp.where(c > 0, h_ref[0], 0.0)
        return _silu(_causal_conv(halo, x_ref[0], w_ref[...]))

    q_all = conv_silu(qh_ref, q_ref, cwq_ref)
    k_all = conv_silu(kh_ref, k_ref, cwk_ref)
    v_all = conv_silu(vh_ref, v_ref, cwv_ref)
    sm = sm_ref[0]
    lane = lax.broadcasted_iota(jnp.int32, sm.shape, 1)
    ri = lax.broadcasted_iota(jnp.int32, (rows, rows), 0)
    ci = lax.broadcasted_iota(jnp.int32, (rows, rows), 1)
    same_chunk = (ri // ch) == (ci // ch)
    incl = same_chunk & (ri >= ci)
    strict = same_chunk & (ri > ci)
    eye = (ri == ci).astype(F32)
    for j in range(heads_here):
        hd = pl.program_id(1) * heads_here + j
        hs = slice(j * dk, (j + 1) * dk)
        _dn_head(hd, q_all[:, hs], k_all[:, hs], v_all[:, hs], z_ref, sm, lane, smt_ref,
                 alog_ref, dtb_ref, nw_ref, o_ref, state_ref, j, hs, incl, strict, eye, n_heads)


def _dn_head(hd, q, k, v, z_ref, sm, lane, smt_ref, alog_ref, dtb_ref, nw_ref, o_ref, state_ref,
             j, hs, incl, strict, eye, n_heads):
    rows, dk = q.shape
    ch = DN_CHUNK
    q = _l2_normalize(q) * (dk ** -0.5)
    k = _l2_normalize(k)
    a_scale = -jnp.exp(jnp.full((1, 1), alog_ref[hd], F32))
    dtb = dtb_ref[hd]
    a_col = jnp.sum(jnp.where(lane == hd, sm, 0.0), axis=1, keepdims=True)
    b_col = jnp.sum(jnp.where(lane == hd + n_heads, sm, 0.0), axis=1, keepdims=True)
    beta = _sigmoid(b_col)
    gc_col = _segment_cumsum(a_scale * _softplus(a_col + dtb), 0, ch)
    a_row = smt_ref[pl.ds(hd, 1), :]
    gc_row = _segment_cumsum(a_scale * _softplus(a_row + dtb), 1, ch)

    n_ch = rows // ch
    decay = jnp.where(incl, jnp.exp(jnp.where(incl, gc_col - gc_row, 0.0)), 0.0)
    kb = k * beta
    kb16, k16 = kb.astype(BF16), k.astype(BF16)
    lower = jnp.where(strict, _dot_nt(kb16, k16) * decay, 0.0)
    p = -lower
    t_inv = eye + p
    m = 2
    while m < ch:
        p16 = p.astype(BF16)
        p = _dot(p16, p16)
        t_inv = t_inv + _dot(t_inv.astype(BF16), p.astype(BF16))
        m *= 2
    t16 = t_inv.astype(BF16)
    eg = jnp.exp(gc_col)
    u16 = _dot(t16, (v * beta).astype(BF16)).astype(BF16)
    w16 = _dot(t16, (kb * eg).astype(BF16)).astype(BF16)
    attn16 = (_dot_nt(q.astype(BF16), k16) * decay).astype(BF16)
    g_ends = [gc_col[(n + 1) * ch - 1:(n + 1) * ch] for n in range(n_ch)]
    g_end_rows = jnp.concatenate([jnp.broadcast_to(g, (ch, 1)) for g in g_ends], axis=0)
    kg16 = (k * jnp.exp(g_end_rows - gc_col)).astype(BF16)
    qp16 = (q * eg - _dot(attn16, w16)).astype(BF16)
    au = _dot(attn16, u16)
    kws, cs = [], []
    for n in range(n_ch):
        sl = slice(n * ch, (n + 1) * ch)
        kws.append(_dot_tn(kg16[sl], w16[sl]).astype(BF16))
        cs.append(_dot_tn(kg16[sl], u16[sl]))
    state = state_ref[j]
    outs = []
    for n in range(n_ch):
        sl = slice(n * ch, (n + 1) * ch)
        s16 = state.astype(BF16)
        outs.append(_dot(qp16[sl], s16) + au[sl])
        state = state * jnp.exp(g_ends[n]) - _dot(kws[n], s16) + cs[n]
    state_ref[j] = state
    o = jnp.concatenate(outs, axis=0) if len(outs) > 1 else outs[0]
    o = o * lax.rsqrt(jnp.mean(o * o, axis=-1, keepdims=True) + 1e-6) * nw_ref[...]
    o_ref[0, :, hs] = (o * _silu(z_ref[0, :, hs])).astype(o_ref.dtype)


DN_HEADS_PER_STEP = 4


def _deltanet(h3, small_t, conv_w, a_log, dt_bias, norm_w, *, n_heads, col0, small_col, rows=DN_ROWS):
    bsz, t_len, _ = h3.shape
    rows = min(rows, t_len)
    hps = DN_HEADS_PER_STEP
    dk = DN_HEAD_DIM
    wblk = hps * dk
    hb = rows // V7X_SUBLANES
    nblk = t_len // rows
    cb = col0 // wblk
    per = n_heads // hps
    smb = small_col // 256

    def halo(off):
        return pl.BlockSpec((1, V7X_SUBLANES, wblk),
                            lambda b, h, c: (b, jnp.maximum(c * hb - 1, 0), cb + off * per + h))

    def cur(off):
        return pl.BlockSpec((1, rows, wblk), lambda b, h, c: (b, c, cb + off * per + h))

    def convw(off):
        return pl.BlockSpec((DN_CONV, wblk), lambda b, h, c: (0, off * per + h))

    smem = pl.BlockSpec(memory_space=pltpu.SMEM)
    return pl.pallas_call(
        functools.partial(_dn_kernel, n_heads=n_heads),
        grid=(bsz, per, nblk),
        in_specs=[
            smem, smem,
            halo(0), cur(0), halo(1), cur(1), halo(2), cur(2), cur(3),
            pl.BlockSpec((1, rows, 256), lambda b, h, c: (b, c, smb)),
            pl.BlockSpec((small_t.shape[0], rows), lambda b, h, c: (0, b * nblk + c)),
            convw(0), convw(1), convw(2),
            pl.BlockSpec((1, dk), lambda b, h, c: (0, 0)),
        ],
        out_specs=pl.BlockSpec((1, rows, wblk), lambda b, h, c: (b, c, h)),
        out_shape=jax.ShapeDtypeStruct((bsz, t_len, n_heads * dk), BF16),
        scratch_shapes=[pltpu.VMEM((hps, dk, dk), F32)],
        compiler_params=_params(("arbitrary", "arbitrary", "arbitrary")),
        name="deltanet",
    )(a_log, dt_bias, h3, h3, h3, h3, h3, h3, h3, h3, small_t,
      conv_w, conv_w, conv_w, norm_w.reshape(1, dk))


NSA_QB = 128
NSA_TK = 512
NSA_MASK_BIG = -1e30
NSA_BIAS_TILES = -(-(REL_MAX_DIST + NSA_QB - 1) // NSA_QB) + 1


def _rel_bucket(dist):
    n = jnp.maximum(dist, 0)
    exact = REL_BUCKETS // 2
    big = exact + (jnp.log(jnp.maximum(n, 1).astype(jnp.float32) / exact)
                   / math.log(REL_MAX_DIST / exact) * (REL_BUCKETS - exact)).astype(jnp.int32)
    return jnp.where(n < exact, n, jnp.minimum(big, REL_BUCKETS - 1))


def _bias_kernel(rb_ref, bk_ref, o_ref):
    bk = bk_ref[...]
    for h in range(o_ref.shape[0]):
        acc = jnp.zeros(bk.shape, F32)
        for b in range(REL_BUCKETS):
            acc = jnp.where(bk == b, rb_ref[b, h], acc)
        o_ref[h] = acc


def _bias_table(rel_bias, buckets, rows=256):
    m, w = buckets.shape
    n_heads = rel_bias.shape[1]
    rows = min(rows, m)
    return pl.pallas_call(
        _bias_kernel,
        grid=(m // rows,),
        in_specs=[pl.BlockSpec(memory_space=pltpu.SMEM),
                  pl.BlockSpec((rows, w), lambda i: (i, 0))],
        out_specs=pl.BlockSpec((n_heads, rows, w), lambda i: (0, i, 0)),
        out_shape=jax.ShapeDtypeStruct((n_heads, m, w), F32),
        compiler_params=_params(("arbitrary",)),
        name="nsa_bias_table",
    )(rel_bias, buckets)


def _cmp_kernel(z_ref, pe_ref, w1_ref, w2_ref, o_ref):
    z = z_ref[0, 0]
    r, half = z.shape
    w1 = w1_ref[0]
    top = _dot(z, w1[:half])
    bot = _dot(z, w1[half:])
    bias = _dot(pe_ref[0], w1)[0:1]
    pre = top + pltpu.roll(bot, r - 1, 0) + bias
    o_ref[0, 0] = _dot(_gelu_tanh(pre).astype(BF16), w2_ref[0]).astype(o_ref.dtype)


def _nsa_compress(z, pe, w1, w2):
    bsz, _, r, half = z.shape
    d = w2.shape[-1]
    return pl.pallas_call(
        _cmp_kernel,
        grid=(bsz, 4),
        in_specs=[
            pl.BlockSpec((1, 1, r, half), lambda b, j: (b, j, 0, 0)),
            pl.BlockSpec((1, V7X_SUBLANES, 2 * half), lambda b, j: (j // 2, 0, 0)),
            pl.BlockSpec((1, 2 * half, w1.shape[-1]), lambda b, j: (j // 2, 0, 0)),
            pl.BlockSpec((1, w2.shape[1], d), lambda b, j: (j // 2, 0, 0)),
        ],
        out_specs=pl.BlockSpec((1, 1, r, d), lambda b, j: (b, j, 0, 0)),
        out_shape=jax.ShapeDtypeStruct((bsz, 4, r, d), BF16),
        compiler_params=_params(("arbitrary", "arbitrary")),
        name="nsa_compress",
    )(z, pe, w1, w2)


def _masked_exp(logits, mask):
    logits = jnp.where(mask, logits, NEG)
    m = jnp.max(logits, axis=-1, keepdims=True)
    e = jnp.where(mask, jnp.exp(logits - m), 0.0)
    return e, 1.0 / jnp.maximum(jnp.sum(e, axis=-1, keepdims=True), 1e-30)


def _nsa_kernel(q_ref, sm_ref, kc_ref, vc_ref, bc_ref, bt_ref, ovt_ref,
                ks_ref, vs_ref, kw_ref, vw_ref, o_ref, *, n_cmp, n_top, gate_col):
    g = pl.program_id(1)
    qb = pl.program_id(2)
    qr = NSA_QB
    d = NSA_HEAD_DIM
    hpg = q_ref.shape[2] // d
    s0 = qb * qr
    q = q_ref[0] * (d ** -0.5)
    q4 = jnp.concatenate([q[:, h * d:(h + 1) * d] for h in range(hpg)], axis=0).astype(BF16)
    t_col = s0 + lax.broadcasted_iota(jnp.int32, (qr, 1), 0)
    t4 = jnp.concatenate([t_col] * hpg, axis=0)

    r = kc_ref.shape[2]
    n_sel = ovt_ref.shape[0]
    sc = _dot_nt(q4, kc_ref[0, 0]) + bc_ref[...].reshape(hpg * qr, r)
    n_row = lax.broadcasted_iota(jnp.int32, (1, r), 1)
    cmp_end = n_row * NSA_CMP_STRIDE + (NSA_CMP_BLOCK - 1)
    e4, inv4 = _masked_exp(sc, (cmp_end <= t4) & (n_row < n_cmp))
    p4 = e4 * inv4
    o_cmp = _dot(p4.astype(BF16), vc_ref[0, 0])

    p_sum = p4[0:qr]
    for h in range(1, hpg):
        p_sum = p_sum + p4[h * qr:(h + 1) * qr]
    p_hi = p_sum.astype(BF16)
    p_lo = (p_sum - p_hi.astype(F32)).astype(BF16)
    ovt = ovt_ref[...]
    imp = _dot_nt(ovt, p_hi) + _dot_nt(ovt, p_lo)
    j_col = lax.broadcasted_iota(jnp.int32, (n_sel, 1), 0)
    t_row = s0 + lax.broadcasted_iota(jnp.int32, (1, qr), 1)
    q_blk = t_row // NSA_SEL_BLOCK
    forced = (j_col == 0) | (j_col == q_blk) | (j_col == q_blk - 1)
    score = jnp.where(j_col * NSA_SEL_BLOCK <= t_row,
                      jnp.where(forced, FORCE_SCORE, imp), -FORCE_SCORE)
    rank = jnp.zeros((n_sel, qr), jnp.int32)
    for k in range(n_sel):
        row = score[k:k + 1]
        ahead = (row > score) | ((row == score) & (j_col > k))
        rank = rank + ahead.astype(jnp.int32)
    sel_t = (rank < n_top).astype(BF16)

    tk = NSA_TK
    key_lane = lax.broadcasted_iota(jnp.int32, (1, tk), 1)
    nb = bt_ref.shape[0]
    ii = lax.broadcasted_iota(jnp.int32, (n_sel, n_sel), 0)
    jj = lax.broadcasted_iota(jnp.int32, (n_sel, n_sel), 1)
    not_sel = _dot_tn((1.0 - sel_t.astype(F32)).astype(BF16), (ii == jj).astype(BF16))
    q_aug = jnp.concatenate([q4, jnp.concatenate([not_sel.astype(BF16)] * hpg, axis=0)], axis=1)

    def sel_step(j, carry, causal):
        m_run, l_run, acc = carry
        k0 = pl.multiple_of(j * tk, tk)
        kt = ks_ref[0, 0, pl.ds(k0, tk), :]
        vt = vs_ref[0, 0, pl.ds(k0, tk), :]
        bias = jnp.concatenate(
            [bt_ref[jnp.clip(qb - (tk // qr) * j - mm, 0, nb - 1)].reshape(hpg * qr, qr)
             for mm in range(tk // qr)], axis=1)
        s = _dot_nt(q_aug, kt) + bias
        if causal:
            off = jnp.where(k0 + key_lane <= t_col, 0.0, NEG)
            s = s + jnp.concatenate([off] * hpg, axis=0)
        m_new = jnp.maximum(m_run, jnp.max(s, axis=-1, keepdims=True))
        scale = jnp.exp(m_run - m_new)
        e = jnp.exp(s - m_new)
        l_new = l_run * scale + jnp.sum(e, axis=-1, keepdims=True)
        acc = acc * scale + _dot(e.astype(BF16), vt)
        return m_new, l_new, acc

    init = (jnp.full((hpg * qr, 1), NEG, F32), jnp.zeros((hpg * qr, 1), F32),
            jnp.zeros((hpg * qr, d), F32))
    last = (s0 + qr - 1) // tk
    carry = lax.fori_loop(0, last, lambda j, c: sel_step(j, c, False), init)
    _, l_sel, acc_sel = sel_step(last, carry, True)
    o_sel = acc_sel / jnp.maximum(l_sel, 1e-30)

    n_wt = NSA_WINDOW // qr + 1
    kws, vws, biases, masks = [], [], [], []
    i_col = lax.broadcasted_iota(jnp.int32, (qr, 1), 0)
    j_row = lax.broadcasted_iota(jnp.int32, (1, qr), 1)
    for mm in range(n_wt):
        k0 = s0 - NSA_WINDOW + mm * qr
        k0c = pl.multiple_of(jnp.maximum(k0, 0), qr)
        kws.append(kw_ref[0, 0, pl.ds(k0c, qr), :])
        vws.append(vw_ref[0, 0, pl.ds(k0c, qr), :])
        biases.append(bt_ref[n_wt - 1 - mm].reshape(hpg * qr, qr))
        dist = (n_wt - 1 - mm) * qr + i_col - j_row
        masks.append((dist >= 0) & (dist < NSA_WINDOW) & (k0 >= 0))
    sw = _dot_nt(q4, jnp.concatenate(kws, axis=0)) + jnp.concatenate(biases, axis=1)
    mw = jnp.concatenate(masks, axis=1)
    ew, inv_w = _masked_exp(sw, jnp.concatenate([mw] * hpg, axis=0))
    o_win = _dot(ew.astype(BF16), jnp.concatenate(vws, axis=0)) * inv_w

    sm = sm_ref[0]
    lane = lax.broadcasted_iota(jnp.int32, sm.shape, 1)

    def gate(h, branch):
        col = gate_col + 3 * (g * hpg + h) + branch
        return _sigmoid(jnp.sum(jnp.where(lane == col, sm, 0.0), axis=1, keepdims=True))

    outs = []
    for h in range(hpg):
        sl = slice(h * qr, (h + 1) * qr)
        outs.append(gate(h, 0) * o_cmp[sl] + gate(h, 1) * o_sel[sl] + gate(h, 2) * o_win[sl])
    o_ref[0] = jnp.concatenate(outs, axis=1).astype(o_ref.dtype)


def _nsa_attention(h3, kv_t, ks_aug, cmp_kv, bias_c, bias_t, ovt, *, q_col, small_col, gate_col, n_cmp):
    bsz, t_len, _ = h3.shape
    g_n = NSA_KV_GROUPS
    d = NSA_HEAD_DIM
    n_heads = bias_c.shape[0]
    hpg = n_heads // g_n
    qw = hpg * d
    r = cmp_kv.shape[2]
    n_sel = t_len // NSA_SEL_BLOCK
    nb = bias_t.shape[0]
    kv_spec = lambda part: pl.BlockSpec((1, 1, t_len, d), lambda b, g, q: (b, 2 * part + g, 0, 0))
    return pl.pallas_call(
        functools.partial(_nsa_kernel, n_cmp=n_cmp, n_top=min(NSA_TOP_BLOCKS, n_sel),
                          gate_col=gate_col),
        grid=(bsz, g_n, t_len // NSA_QB),
        in_specs=[
            pl.BlockSpec((1, NSA_QB, qw), lambda b, g, q: (b, q, q_col // qw + g)),
            pl.BlockSpec((1, NSA_QB, 256), lambda b, g, q: (b, q, small_col // 256)),
            pl.BlockSpec((1, 1, r, d), lambda b, g, q: (b, g, 0, 0)),
            pl.BlockSpec((1, 1, r, d), lambda b, g, q: (b, g_n + g, 0, 0)),
            pl.BlockSpec((hpg, NSA_QB, r), lambda b, g, q: (g, q, 0)),
            pl.BlockSpec((nb, hpg, NSA_QB, NSA_QB), lambda b, g, q: (0, g, 0, 0)),
            pl.BlockSpec((n_sel, r), lambda b, g, q: (0, 0)),
            pl.BlockSpec((1, 1, t_len, ks_aug.shape[3]), lambda b, g, q: (b, g, 0, 0)),
            kv_spec(3), kv_spec(4), kv_spec(5),
        ],
        out_specs=pl.BlockSpec((1, NSA_QB, qw), lambda b, g, q: (b, q, g)),
        out_shape=jax.ShapeDtypeStruct((bsz, t_len, n_heads * d), BF16),
        compiler_params=_params(("arbitrary", "arbitrary", "arbitrary")),
        name="nsa_attention",
    )(h3, h3, cmp_kv, cmp_kv, bias_c, bias_t, ovt, ks_aug, kv_t, kv_t, kv_t)


def _outproj_kernel(x_ref, ya_ref, yb_ref, yc_ref, wa_ref, wb_ref, wc_ref, g_ref, b_ref, o_ref,
                    *, alpha):
    mix = _dot(ya_ref[...], wa_ref[...]) + _dot(yb_ref[...], wb_ref[...]) + _dot(yc_ref[...], wc_ref[...])
    o_ref[...] = _layer_norm(alpha * x_ref[...] + mix, g_ref[...], b_ref[...])


def _outproj_ln(x2d, ya, yb, yc, w_out, g, b, alpha, tm=256):
    n, d = x2d.shape
    wa_n, wb_n, wc_n = ya.shape[1], yb.shape[1], yc.shape[1]
    w16 = w_out.astype(BF16)
    row = lambda w: pl.BlockSpec((tm, w), lambda i: (i, 0))
    full = lambda r: pl.BlockSpec((r, d), lambda i: (0, 0))
    return pl.pallas_call(
        functools.partial(_outproj_kernel, alpha=alpha),
        grid=(n // tm,),
        in_specs=[row(d), row(wa_n), row(wb_n), row(wc_n),
                  full(wa_n), full(wb_n), full(wc_n), full(1), full(1)],
        out_specs=row(d),
        out_shape=jax.ShapeDtypeStruct((n, d), F32),
        compiler_params=_params(("arbitrary",)),
        name="outproj_ln",
    )(x2d, ya, yb, yc, w16[:wa_n], w16[wa_n:wa_n + wb_n], w16[wa_n + wb_n:],
      g.reshape(1, d), b.reshape(1, d))


def _mm_kernel(x_ref, w_ref, o_ref):
    o_ref[...] = _dot(x_ref[...].astype(BF16), w_ref[...]).astype(o_ref.dtype)


def _matmul(x2d, w16, out_dtype, tm=256):
    n, d = x2d.shape
    cols = w16.shape[1]
    tm = min(tm, n)
    return pl.pallas_call(
        _mm_kernel,
        grid=(n // tm,),
        in_specs=[pl.BlockSpec((tm, d), lambda i: (i, 0)),
                  pl.BlockSpec((d, cols), lambda i: (0, 0))],
        out_specs=pl.BlockSpec((tm, cols), lambda i: (i, 0)),
        out_shape=jax.ShapeDtypeStruct((n, cols), out_dtype),
        compiler_params=_params(("arbitrary",)),
        name="mem_kv_proj",
    )(x2d, w16)


def _xattn_kernel(x_ref, kv_ref, wq_ref, wo_ref, g_ref, b_ref, o_ref, o16_ref, oslab_ref, *, alpha):
    x = x_ref[...]
    q = _dot(x.astype(BF16), wq_ref[...])
    kv = kv_ref[0]
    width = wq_ref.shape[1]
    hd = XA_HEAD_DIM
    outs = []
    for h in range(width // hd):
        qh = q[:, h * hd:(h + 1) * hd].astype(BF16)
        kh = kv[:, h * hd:(h + 1) * hd]
        vh = kv[:, width + h * hd:width + (h + 1) * hd]
        logits = _dot_nt(qh, kh) * (hd ** -0.5)
        m = jnp.max(logits, axis=-1, keepdims=True)
        e = jnp.exp(logits - m)
        p = e / jnp.sum(e, axis=-1, keepdims=True)
        outs.append(_dot(p.astype(BF16), vh))
    o = jnp.concatenate(outs, axis=1).astype(BF16)
    y = _layer_norm(alpha * x + _dot(o, wo_ref[...]), g_ref[...], b_ref[...])
    o_ref[...] = y
    o16_ref[...] = y.astype(BF16)
    tm = y.shape[0]
    cpr = y.shape[1] // V7X_LANES
    for c in range(cpr):
        oslab_ref[pl.ds(c, tm, stride=cpr), :] = y[:, c * V7X_LANES:(c + 1) * V7X_LANES]


def _cross_attention_ln(x2d, kv, wq, wo, g, b, alpha, t_len, tm=256):
    n, d = x2d.shape
    width = wq.shape[1]
    m_len = kv.shape[1]
    per_b = t_len // tm
    cpr = d // V7X_LANES
    return pl.pallas_call(
        functools.partial(_xattn_kernel, alpha=alpha),
        grid=(n // tm,),
        in_specs=[
            pl.BlockSpec((tm, d), lambda i: (i, 0)),
            pl.BlockSpec((1, m_len, 2 * width), lambda i: (i // per_b, 0, 0)),
            pl.BlockSpec((d, width), lambda i: (0, 0)),
            pl.BlockSpec((width, d), lambda i: (0, 0)),
            pl.BlockSpec((1, d), lambda i: (0, 0)),
            pl.BlockSpec((1, d), lambda i: (0, 0)),
        ],
        out_specs=[pl.BlockSpec((tm, d), lambda i: (i, 0)), pl.BlockSpec((tm, d), lambda i: (i, 0)),
                   pl.BlockSpec((tm * cpr, V7X_LANES), lambda i: (i, 0))],
        out_shape=[jax.ShapeDtypeStruct((n, d), F32), jax.ShapeDtypeStruct((n, d), BF16),
                   jax.ShapeDtypeStruct((n * cpr, V7X_LANES), F32)],
        compiler_params=_params(("arbitrary",)),
        name="cross_attention_ln",
    )(x2d, kv, wq.astype(BF16), wo.astype(BF16), g.reshape(1, d), b.reshape(1, d))


def _rank_rows(v, idx_col):
    rank = jnp.zeros(v.shape, jnp.int32)
    for k in range(v.shape[0]):
        row = v[k:k + 1]
        rank = rank + ((row > v) | ((row == v) & (idx_col > k))).astype(jnp.int32)
    return rank


def _router_kernel(x_ref, w_ref, b_ref, e_ref, g_ref, cnt_ref):
    n_exp = w_ref.shape[0]
    gsz = n_exp // N_GROUPS
    logits = lax.dot_general(w_ref[...], x_ref[...], (((1,), (1,)), ((), ())),
                             preferred_element_type=F32, precision=lax.Precision.HIGHEST)
    scores = _sigmoid(logits)
    choice = scores + b_ref[...]
    tm = scores.shape[1]
    sub = lax.broadcasted_iota(jnp.int32, (gsz, 1), 0)
    grp_rows = []
    for gi in range(N_GROUPS):
        slab = choice[gi * gsz:(gi + 1) * gsz]
        m1 = jnp.max(slab, axis=0, keepdims=True)
        first = jnp.min(jnp.where(slab == m1, sub, gsz), axis=0, keepdims=True)
        m2 = jnp.max(jnp.where(sub == first, NEG, slab), axis=0, keepdims=True)
        grp_rows.append(m1 + m2)
    grp = jnp.concatenate(grp_rows, axis=0)
    g_idx = lax.broadcasted_iota(jnp.int32, (N_GROUPS, 1), 0)
    grp_ok = _rank_rows(grp, g_idx) < TOPK_GROUPS
    ok = jnp.concatenate([jnp.broadcast_to(grp_ok[gi:gi + 1], (gsz, tm)) for gi in range(N_GROUPS)], axis=0)
    e_idx = lax.broadcasted_iota(jnp.int32, (n_exp, 1), 0)
    rank = _rank_rows(jnp.where(ok, choice, -1e9), e_idx)
    denom = jnp.sum(jnp.where(rank < TOP_K, scores, 0.0), axis=0, keepdims=True)
    e_rows, g_rows = [], []
    for k in range(TOP_K):
        hit = rank == k
        e_rows.append(jnp.sum(jnp.where(hit, e_idx, 0), axis=0, keepdims=True))
        g_rows.append(jnp.sum(jnp.where(hit, scores, 0.0), axis=0, keepdims=True) / denom * ROUTED_SCALE)
    e_ref[...] = jnp.concatenate(e_rows, axis=0)
    g_ref[...] = jnp.concatenate(g_rows, axis=0)

    @pl.when(pl.program_id(0) == 0)
    def _():
        cnt_ref[...] = jnp.zeros_like(cnt_ref)

    cnt_ref[...] += jnp.sum((rank < TOP_K).astype(jnp.int32), axis=1, keepdims=True)


def _router(x2d, w_router, bias, tm=512):
    n, d = x2d.shape
    n_exp = w_router.shape[1]
    return pl.pallas_call(
        _router_kernel,
        grid=(n // tm,),
        in_specs=[pl.BlockSpec((tm, d), lambda i: (i, 0)),
                  pl.BlockSpec((n_exp, d), lambda i: (0, 0)),
                  pl.BlockSpec((n_exp, 1), lambda i: (0, 0))],
        out_specs=[pl.BlockSpec((TOP_K, tm), lambda i: (0, i)), pl.BlockSpec((TOP_K, tm), lambda i: (0, i)),
                   pl.BlockSpec((n_exp, 1), lambda i: (0, 0))],
        out_shape=[jax.ShapeDtypeStruct((TOP_K, n), jnp.int32), jax.ShapeDtypeStruct((TOP_K, n), F32),
                   jax.ShapeDtypeStruct((n_exp, 1), jnp.int32)],
        compiler_params=_params(("arbitrary",)),
        name="moe_router",
    )(x2d, w_router.T, bias.reshape(n_exp, 1))


def _expert_kernel(e_ref, b_ref, lo_ref, hi_ref, first_ref,
                   tok_ref, dst_ref, tok_next_ref, x_hbm, wt_ref, wg_ref, wu_ref, wd_ref, y_hbm,
                   xbuf, ostage, obuf, wg16, wu16, wd16, gsem, ssem):
    i = pl.program_id(0)
    n_items = pl.num_programs(0)
    slot = i % 2
    rows = wt_ref.shape[0]
    cpr = xbuf.shape[1] // rows
    lo, hi = lo_ref[i], hi_ref[i]

    def slab(ref, r):
        return ref.at[pl.ds(pl.multiple_of(r * cpr, cpr), cpr)]

    def for_rows(r0, r1, start_row):
        n_groups = (r1 - r0) // MOE_DMA_UNROLL

        def group(g, _):
            base = r0 + g * MOE_DMA_UNROLL
            for u in range(MOE_DMA_UNROLL):
                start_row(base + u, u % 2)
            return 0

        def single(r, _):
            start_row(r, 0)
            return 0

        lax.fori_loop(0, n_groups, group, 0)
        lax.fori_loop(r0 + n_groups * MOE_DMA_UNROLL, r1, single, 0)

    def start_gather(idx_ref, r0, r1, s):
        def start_row(r, priority):
            pltpu.make_async_copy(slab(x_hbm, idx_ref[0, 0, r]), slab(xbuf.at[s], r),
                                  gsem.at[s]).start(priority=priority)

        for_rows(r0, r1, start_row)

    def wait_gather(n, s):
        @pl.when(n > 0)
        def _():
            pltpu.make_async_copy(x_hbm.at[pl.ds(0, n * cpr)], xbuf.at[s, pl.ds(0, n * cpr)],
                                  gsem.at[s]).wait()

    def wait_scatter(n, s):
        @pl.when(n > 0)
        def _():
            pltpu.make_async_copy(obuf.at[s, pl.ds(0, n * cpr)], y_hbm.at[pl.ds(0, n * cpr)],
                                  ssem.at[s]).wait()

    @pl.when(i == 0)
    def _():
        xbuf[...] = jnp.zeros_like(xbuf)
        start_gather(tok_ref, lo, hi, 0)

    @pl.when(i + 1 < n_items)
    def _():
        start_gather(tok_next_ref, lo_ref[i + 1], hi_ref[i + 1], 1 - slot)

    @pl.when(first_ref[i] == 1)
    def _():
        wg16[...] = wg_ref[...].astype(BF16)
        wu16[...] = wu_ref[...].astype(BF16)
        wd16[...] = wd_ref[...].astype(BF16)

    wait_gather(hi - lo, slot)

    @pl.when(i >= 2)
    def _():
        wait_scatter(hi_ref[i - 2] - lo_ref[i - 2], slot)

    @pl.when(hi > lo)
    def _():
        xb = jnp.concatenate([xbuf[slot, pl.ds(c, rows, stride=cpr), :] for c in range(cpr)], axis=1)
        xb = xb.astype(BF16)
        hid = _silu(_dot(xb, wg16[...])) * _dot(xb, wu16[...])
        out = _dot(hid.astype(BF16), wd16[...]) * wt_ref[...]
        for c in range(cpr):
            ostage[pl.ds(c, rows, stride=cpr), :] = out[:, c * V7X_LANES:(c + 1) * V7X_LANES]
        obuf[slot] = ostage[...].astype(obuf.dtype)

        def start_row(r, priority):
            pltpu.make_async_copy(slab(obuf.at[slot], r), slab(y_hbm, dst_ref[0, 0, r]),
                                  ssem.at[slot]).start(priority=priority)

        for_rows(lo, hi, start_row)

    @pl.when(i == n_items - 1)
    def _():
        @pl.when(i >= 1)
        def _():
            wait_scatter(hi_ref[i - 1] - lo_ref[i - 1], 1 - slot)

        wait_scatter(hi - lo, slot)


def _experts(x_slab, plan, w_gate, w_up, w_down, layer):
    item_e, item_b, item_lo, item_hi, item_first, row_tok, row_dst, row_w = plan
    d = w_gate.shape[2]
    n = x_slab.shape[0] * V7X_LANES // d
    n_items = item_e.shape[0]
    dff = w_gate.shape[3]
    rows = MOE_ROWS
    n_blk = row_tok.shape[0] // rows
    cpr = d // V7X_LANES
    wspec = lambda r, c: pl.BlockSpec((None, None, r, c), lambda i, e, *_: (layer, e[i], 0, 0))
    cur = lambda i, e, b, *_: (b[i], 0, 0)
    nxt = lambda i, e, b, *_: (b[jnp.minimum(i + 1, n_items - 1)], 0, 0)
    grid_spec = pltpu.PrefetchScalarGridSpec(
        num_scalar_prefetch=5,
        grid=(n_items,),
        in_specs=[
            pl.BlockSpec((1, 1, rows), cur, memory_space=pltpu.SMEM),
            pl.BlockSpec((1, 1, rows), cur, memory_space=pltpu.SMEM),
            pl.BlockSpec((1, 1, rows), nxt, memory_space=pltpu.SMEM),
            pl.BlockSpec(memory_space=pl.ANY),
            pl.BlockSpec((rows, 1), lambda i, e, b, *_: (b[i], 0)),
            wspec(d, dff), wspec(d, dff), wspec(dff, d),
        ],
        out_specs=pl.BlockSpec(memory_space=pl.ANY),
        scratch_shapes=[
            pltpu.VMEM((2, rows * cpr, V7X_LANES), F32), pltpu.VMEM((rows * cpr, V7X_LANES), F32),
            pltpu.VMEM((2, rows * cpr, V7X_LANES), BF16),
            pltpu.VMEM((d, dff), BF16), pltpu.VMEM((d, dff), BF16), pltpu.VMEM((dff, d), BF16),
            pltpu.SemaphoreType.DMA((2,)), pltpu.SemaphoreType.DMA((2,)),
        ],
    )
    tok3 = row_tok.reshape(n_blk, 1, rows)
    return pl.pallas_call(
        _expert_kernel,
        grid_spec=grid_spec,
        out_shape=jax.ShapeDtypeStruct((n * TOP_K * cpr, V7X_LANES), BF16),
        compiler_params=_params(("arbitrary",)),
        name="moe_experts",
    )(item_e, item_b, item_lo, item_hi, item_first,
      tok3, row_dst.reshape(n_blk, 1, rows), tok3,
      x_slab, row_w.reshape(n_blk * rows, 1), w_gate, w_up, w_down)


def _combine_kernel(x_ref, x16_ref, y_ref, wg_ref, wu_ref, wd_ref, g_ref, b_ref, o_ref, acc_ref,
                    *, alpha):
    x16 = x16_ref[...]
    shared = _dot((_silu(_dot(x16, wg_ref[...])) * _dot(x16, wu_ref[...])).astype(BF16), wd_ref[...])
    tm, d = x_ref.shape
    cpr = d // V7X_LANES
    tot = y_ref[0].astype(F32)
    for k in range(1, TOP_K):
        tot = tot + y_ref[k].astype(F32)
    acc_ref[...] = tot
    routed = jnp.concatenate([acc_ref[pl.ds(c, tm, stride=cpr), :] for c in range(cpr)], axis=1)
    o_ref[...] = _layer_norm(alpha * x_ref[...] + (routed + shared), g_ref[...], b_ref[...])


def _moe_combine_ln(x2d, x16, y8, ws_gate, ws_up, ws_down, g, b, alpha, tm=128):
    n, d = x2d.shape
    dff = ws_gate.shape[1]
    cpr = d // V7X_LANES
    row = lambda w: pl.BlockSpec((tm, w), lambda i: (i, 0))
    full = lambda r, c: pl.BlockSpec((r, c), lambda i: (0, 0))
    return pl.pallas_call(
        functools.partial(_combine_kernel, alpha=alpha),
        grid=(n // tm,),
        in_specs=[row(d), row(d), pl.BlockSpec((TOP_K, tm * cpr, V7X_LANES), lambda i: (0, i, 0)),
                  full(d, dff), full(d, dff), full(dff, d), full(1, d), full(1, d)],
        out_specs=row(d),
        out_shape=jax.ShapeDtypeStruct((n, d), F32),
        scratch_shapes=[pltpu.VMEM((tm * cpr, V7X_LANES), F32)],
        compiler_params=_params(("arbitrary",)),
        name="moe_combine_ln",
    )(x2d, x16, y8.reshape(TOP_K, n * cpr, V7X_LANES), ws_gate.astype(BF16), ws_up.astype(BF16),
      ws_down.astype(BF16), g.reshape(1, d), b.reshape(1, d))


def _dispatch_plan(top_e, gate, counts):
    n = top_e.shape[1]
    n_exp = counts.shape[0]
    rows = MOE_ROWS
    n_assign = n * TOP_K
    n_items = n_assign // rows + n_exp
    a_iota = jnp.arange(n_assign, dtype=jnp.int32)
    _, row_dst, row_w = lax.sort((top_e.reshape(-1), a_iota, gate.reshape(-1)), num_keys=1)
    row_tok = row_dst % n
    ends = jnp.cumsum(counts)
    starts = ends - counts
    n_blk_e = jnp.where(counts > 0, (ends - 1) // rows - starts // rows + 1, 0)
    item_end = jnp.cumsum(n_blk_e)
    item_start = item_end - n_blk_e
    t = jnp.arange(n_items, dtype=jnp.int32)
    active = t < item_end[-1]
    item_e = jnp.minimum(jnp.sum((item_end[None, :] <= t[:, None]).astype(jnp.int32), axis=1), n_exp - 1)
    onehot = item_e[:, None] == jnp.arange(n_exp, dtype=jnp.int32)[None, :]
    pick = lambda v: jnp.sum(jnp.where(onehot, v[None, :], 0), axis=1)
    s_e, e_e, i_e = pick(starts), pick(ends), pick(item_start)
    item_b = jnp.clip(s_e // rows + (t - i_e), 0, n_assign // rows - 1)
    item_lo = jnp.where(active, jnp.clip(s_e - item_b * rows, 0, rows), 0)
    item_hi = jnp.where(active, jnp.clip(e_e - item_b * rows, 0, rows), 0)
    item_first = (active & (t == i_e)).astype(jnp.int32)
    i32 = lambda v: v.astype(jnp.int32)
    return (i32(item_e), i32(item_b), i32(item_lo), i32(item_hi), item_first,
            i32(row_tok), i32(row_dst), row_w)


def kernel(x, mem, rel_bias, w_in, rg_conv_w, rg_conv_b, rg_wa, rg_ba, rg_wi, rg_bi, rg_lambda,
           dn_conv_w, dn_a_log, dn_dt_bias, dn_norm_w, nsa_pe_k, nsa_pe_v, nsa_phi_k1, nsa_phi_k2,
           nsa_phi_v1, nsa_phi_v2, w_out, ln1_g, ln1_b, xa_wq, xa_wk, xa_wv, xa_wo, ln2_g, ln2_b,
           moe_router, moe_router_bias, moe_w_gate, moe_w_up, moe_w_down, shared_w_gate, shared_w_up,
           shared_w_down, ln3_g, ln3_b):
    bsz, t_len, d_model = x.shape
    depth = w_in.shape[0]
    n_tok = bsz * t_len
    alpha = (2 * depth) ** 0.25
    rg_w = rg_conv_w.shape[2]
    dn_w = dn_conv_w.shape[2] // 3
    dn_heads = dn_a_log.shape[1]
    nsa_heads = rel_bias.shape[1]
    nsa_w = nsa_heads * NSA_HEAD_DIM
    kv_w = NSA_KV_GROUPS * NSA_HEAD_DIM
    n_exp = moe_router.shape[2]

    sizes = (rg_w, rg_w, dn_w, dn_w, dn_w, dn_w, dn_heads, dn_heads, nsa_w, 6 * kv_w, 3 * nsa_heads)
    offs = [0]
    for s in sizes:
        offs.append(offs[-1] + s)
    col = lambda w, i: w[:, :, offs[i]:offs[i + 1]]
    main = offs[6]
    q_col = main
    small_col = q_col + nsa_w
    kv_col = small_col + 256
    gate_col = 2 * dn_heads
    small = jnp.concatenate([col(w_in, 6), col(w_in, 7), col(w_in, 10)], axis=2)
    small = jnp.pad(small, ((0, 0), (0, 0), (0, 256 - small.shape[2])))
    w_slab = jnp.concatenate([w_in[:, :, :main], col(w_in, 8), small, col(w_in, 9)], axis=2).astype(BF16)
    w_small_t = jnp.swapaxes(jnp.concatenate([col(w_in, 6), col(w_in, 7)], axis=2), 1, 2).astype(BF16)

    r_cmp = t_len // NSA_CMP_STRIDE
    n_cmp = (t_len - NSA_CMP_BLOCK) // NSA_CMP_STRIDE + 1
    n_sel = t_len // NSA_SEL_BLOCK
    cmp_start = jnp.arange(r_cmp) * NSA_CMP_STRIDE
    cmp_end = cmp_start + NSA_CMP_BLOCK - 1
    sel_start = jnp.arange(n_sel) * NSA_SEL_BLOCK
    ovt = ((cmp_start[None, :] < sel_start[:, None] + NSA_SEL_BLOCK)
           & (cmp_end[None, :] >= sel_start[:, None])
           & (jnp.arange(r_cmp)[None, :] < n_cmp)).astype(BF16)
    blk_ind = jnp.where(jnp.arange(t_len)[:, None] // NSA_SEL_BLOCK == jnp.arange(n_sel)[None, :],
                        NSA_MASK_BIG, 0.0).astype(BF16)
    bucket_c = _rel_bucket(jnp.arange(t_len)[:, None] - cmp_end[None, :]).astype(jnp.int32)
    tile = jnp.arange(NSA_BIAS_TILES * NSA_QB)[:, None] - jnp.arange(NSA_QB)[None, :]
    bucket_t = _rel_bucket(tile).astype(jnp.int32)
    bias_c = _bias_table(rel_bias, bucket_c)
    bias_t = _bias_table(rel_bias, bucket_t).reshape(nsa_heads, NSA_BIAS_TILES, NSA_QB, NSA_QB)
    bias_t = jnp.swapaxes(bias_t, 0, 1)

    cmp_in = NSA_CMP_BLOCK * NSA_HEAD_DIM
    x2d = x.reshape(n_tok, d_model)
    mem2d = mem.reshape(-1, d_model)
    for l in range(depth):
        h, small_t = _inproj(x2d, w_slab, w_small_t, l)
        h3 = h.reshape(bsz, t_len, -1)
        y_a = _rglru(h3, rg_conv_w[l], rg_conv_b[l], rg_wa[l], rg_wi[l], rg_ba[l], rg_bi[l],
                     rg_lambda[l], rg_w)
        y_b = _deltanet(h3, small_t, dn_conv_w[l], dn_a_log[l], dn_dt_bias[l], dn_norm_w[l],
                        n_heads=dn_heads, col0=2 * rg_w, small_col=small_col)
        kv_t = h3[:, :, kv_col:].reshape(bsz, t_len, 6 * NSA_KV_GROUPS, NSA_HEAD_DIM)
        kv_t = jnp.swapaxes(kv_t, 1, 2).astype(BF16)
        z = kv_t[:, :2 * NSA_KV_GROUPS].reshape(bsz, 2 * NSA_KV_GROUPS, r_cmp, cmp_in // 2)
        pe = jnp.stack([nsa_pe_k[l], nsa_pe_v[l]]).reshape(2, 1, cmp_in)
        pe = jnp.broadcast_to(pe, (2, V7X_SUBLANES, cmp_in)).astype(BF16)
        w1 = jnp.stack([nsa_phi_k1[l], nsa_phi_v1[l]]).astype(BF16)
        w2 = jnp.stack([nsa_phi_k2[l], nsa_phi_v2[l]]).astype(BF16)
        cmp_kv = _nsa_compress(z, pe, w1, w2)
        ks_aug = jnp.concatenate(
            [kv_t[:, 2 * NSA_KV_GROUPS:3 * NSA_KV_GROUPS],
             jnp.broadcast_to(blk_ind, (bsz, NSA_KV_GROUPS, t_len, n_sel))], axis=-1)
        y_c = _nsa_attention(h3, kv_t, ks_aug, cmp_kv, bias_c, bias_t, ovt, q_col=q_col,
                             small_col=small_col, gate_col=gate_col, n_cmp=n_cmp)
        x2d = _outproj_ln(x2d, y_a.reshape(n_tok, -1), y_b.reshape(n_tok, -1), y_c.reshape(n_tok, -1),
                          w_out[l], ln1_g[l], ln1_b[l], alpha)
        w_kv = jnp.concatenate([xa_wk[l], xa_wv[l]], axis=1).astype(BF16)
        kv_mem = _matmul(mem2d, w_kv, BF16).reshape(bsz, mem.shape[1], -1)
        x2d, x16, x_slab = _cross_attention_ln(x2d, kv_mem, xa_wq[l], xa_wo[l], ln2_g[l], ln2_b[l],
                                               alpha, t_len)
        top_e, gate, counts = _router(x2d, moe_router[l], moe_router_bias[l])
        plan = _dispatch_plan(top_e, gate, counts.reshape(-1))
        y8 = _experts(x_slab, plan, moe_w_gate, moe_w_up, moe_w_down, l)
        x2d = _moe_combine_ln(x2d, x16, y8, shared_w_gate[l], shared_w_up[l], shared_w_down[l],
                              ln3_g[l], ln3_b[l], alpha)
    return x2d.reshape(bsz, t_len, d_model)
```

```python
import functools
import math

import jax
import jax.numpy as jnp
from jax import lax
from jax.experimental import pallas as pl
from jax.experimental.pallas import tpu as pltpu

F32 = jnp.float32
BF16 = jnp.bfloat16

LN_EPS = 1e-5
NEG = -1e30

RG_BLOCKS = 4
RG_CONV = 4
RG_C = 8.0
DN_HEAD_DIM = 128
DN_CONV = 4
DN_CHUNK = 64
DN_ROWS = 256
NSA_HEAD_DIM = 64
NSA_KV_GROUPS = 2
NSA_CMP_BLOCK = 32
NSA_CMP_STRIDE = 16
NSA_SEL_BLOCK = 64
NSA_TOP_BLOCKS = 16
NSA_WINDOW = 512
FORCE_SCORE = 1e4
REL_BUCKETS = 32
REL_MAX_DIST = 1024
XA_HEADS = 4
XA_HEAD_DIM = 128
TOP_K = 8
N_GROUPS = 8
TOPK_GROUPS = 4
ROUTED_SCALE = 2.5
MOE_ROWS = 256
MOE_GATHER_UNROLL = 32
MOE_SCATTER_UNROLL = 8
MOE_XBUF_PITCH = 24

V7X_LANES = 128
V7X_SUBLANES = 8
V7X_VMEM_LIMIT = 56 * 1024 * 1024


def _params(semantics, vmem=V7X_VMEM_LIMIT):
    return pltpu.CompilerParams(dimension_semantics=semantics, vmem_limit_bytes=vmem)


def _sigmoid(x):
    return 0.5 * (1.0 + jnp.tanh(0.5 * x))


def _softplus(x):
    return jnp.maximum(x, 0.0) + jnp.log1p(jnp.exp(-jnp.abs(x)))


def _silu(x):
    return x * _sigmoid(x)


def _gelu_tanh(x):
    c = math.sqrt(2.0 / math.pi)
    return x * (0.5 * (1.0 + jnp.tanh(c * (x + 0.044715 * (x * x * x)))))


def _layer_norm(v, g, b):
    mu = jnp.mean(v, axis=-1, keepdims=True)
    vc = v - mu
    var = jnp.mean(vc * vc, axis=-1, keepdims=True)
    return vc * lax.rsqrt(var + LN_EPS) * g + b


def _dot(a, b):
    return jnp.dot(a, b, preferred_element_type=F32)


def _dot_nt(a, b):
    return lax.dot_general(a, b, (((1,), (1,)), ((), ())), preferred_element_type=F32)


def _dot_tn(a, b):
    return lax.dot_general(a, b, (((0,), (0,)), ((), ())), preferred_element_type=F32)


def _inproj_kernel(x_ref, w_ref, wt_ref, o_ref, ot_ref, xb_ref):
    @pl.when(pl.program_id(1) == 0)
    def _():
        xb_ref[...] = x_ref[...].astype(BF16)
        ot_ref[...] = _dot_nt(wt_ref[...], xb_ref[...])

    o_ref[...] = _dot(xb_ref[...], w_ref[...])


def _inproj(x2d, w, wt, layer, tm=1024, tn=512):
    n, d = x2d.shape
    cols = w.shape[2]
    return pl.pallas_call(
        _inproj_kernel,
        grid=(n // tm, cols // tn),
        in_specs=[
            pl.BlockSpec((tm, d), lambda i, j: (i, 0)),
            pl.BlockSpec((None, d, tn), lambda i, j: (layer, 0, j)),
            pl.BlockSpec((None, wt.shape[1], d), lambda i, j: (layer, 0, 0)),
        ],
        out_specs=[
            pl.BlockSpec((tm, tn), lambda i, j: (i, j)),
            pl.BlockSpec((wt.shape[1], tm), lambda i, j: (0, i)),
        ],
        out_shape=[
            jax.ShapeDtypeStruct((n, cols), F32),
            jax.ShapeDtypeStruct((wt.shape[1], n), F32),
        ],
        scratch_shapes=[pltpu.VMEM((tm, d), BF16)],
        compiler_params=_params(("arbitrary", "arbitrary")),
        name="inproj",
    )(x2d, w, wt)


def _shifted_rows(halo, x, s):
    cat = jnp.concatenate([halo, x], axis=0)
    return pltpu.roll(cat, s, 0)[V7X_SUBLANES:]


def _causal_conv(halo, x, w):
    k = w.shape[0]
    y = x * w[k - 1:k]
    for s in range(1, k):
        y = y + _shifted_rows(halo, x, s) * w[k - 1 - s:k - s]
    return y


def _rg_kernel(xh_ref, x_ref, g_ref, cw_ref, cb_ref, wa_ref, wi_ref, ba_ref, bi_ref, lam_ref,
               o_ref, h_ref):
    c = pl.program_id(1)
    x = x_ref[0]
    rows, width = x.shape
    blk = width // RG_BLOCKS
    halo = jnp.where(c > 0, xh_ref[0], 0.0)
    xc = _causal_conv(halo, x, cw_ref[...]) + cb_ref[...]
    xcb = xc.astype(BF16)
    r_parts, i_parts = [], []
    for n in range(RG_BLOCKS):
        xg = xcb[:, n * blk:(n + 1) * blk]
        r_parts.append(_dot(xg, wa_ref[n]))
        i_parts.append(_dot(xg, wi_ref[n]))
    r = _sigmoid(jnp.concatenate(r_parts, axis=1) + ba_ref[...])
    gi = _sigmoid(jnp.concatenate(i_parts, axis=1) + bi_ref[...])
    log_a = -RG_C * r * _softplus(-lam_ref[...])
    a = jnp.exp(log_a)
    th = jnp.tanh(log_a)
    b = jnp.sqrt(-2.0 * th / (1.0 - th)) * (gi * xc)
    row = lax.broadcasted_iota(jnp.int32, (rows, 1), 0)
    s = 1
    while s < rows:
        a_s = pltpu.roll(a, s, 0)
        b_s = pltpu.roll(b, s, 0)
        m = row >= s
        b = jnp.where(m, a * b_s + b, b)
        a = jnp.where(m, a * a_s, a)
        s *= 2

    @pl.when(c == 0)
    def _():
        h_ref[...] = jnp.zeros_like(h_ref)

    h = a * h_ref[0:1] + b
    h_ref[...] = jnp.broadcast_to(h[rows - 1:rows], h_ref.shape)
    o_ref[0] = (h * _gelu_tanh(g_ref[0])).astype(o_ref.dtype)


def _rglru(h3, cw, cb, wa, wi, ba, bi, lam, width, rows=512):
    bsz, t_len, _ = h3.shape
    rows = min(rows, t_len)
    hb = rows // V7X_SUBLANES
    vec = lambda v: v.reshape(1, width)
    full = lambda shape: pl.BlockSpec(shape, lambda b, c: (0,) * len(shape))
    return pl.pallas_call(
        _rg_kernel,
        grid=(bsz, t_len // rows),
        in_specs=[
            pl.BlockSpec((1, V7X_SUBLANES, width), lambda b, c: (b, jnp.maximum(c * hb - 1, 0), 0)),
            pl.BlockSpec((1, rows, width), lambda b, c: (b, c, 0)),
            pl.BlockSpec((1, rows, width), lambda b, c: (b, c, 1)),
            full((RG_CONV, width)), full((1, width)),
            full(wa.shape), full(wi.shape),
            full((1, width)), full((1, width)), full((1, width)),
        ],
        out_specs=pl.BlockSpec((1, rows, width), lambda b, c: (b, c, 0)),
        out_shape=jax.ShapeDtypeStruct((bsz, t_len, width), BF16),
        scratch_shapes=[pltpu.VMEM((V7X_SUBLANES, width), F32)],
        compiler_params=_params(("arbitrary", "arbitrary")),
        name="rglru",
    )(h3, h3, h3, cw, vec(cb), wa.astype(BF16), wi.astype(BF16), vec(ba), vec(bi), vec(lam))


def _segment_cumsum(v, axis, seg):
    pos = lax.broadcasted_iota(jnp.int32, v.shape, axis) % seg
    s = 1
    while s < seg:
        v = v + jnp.where(pos >= s, pltpu.roll(v, s, axis), 0.0)
        s *= 2
    return v


def _l2_normalize(v):
    return v * lax.rsqrt(jnp.sum(v * v, axis=-1, keepdims=True) + 1e-6)


def _dn_kernel(alog_ref, dtb_ref,
               qh_ref, q_ref, kh_ref, k_ref, vh_ref, v_ref, z_ref, sm_ref, smt_ref,
               cwq_ref, cwk_ref, cwv_ref, nw_ref, o_ref, state_ref, *, n_heads):
    c = pl.program_id(2)
    rows = q_ref.shape[1]
    dk = DN_HEAD_DIM
    heads_here = q_ref.shape[2] // dk
    ch = DN_CHUNK

    @pl.when(c == 0)
    def _():
        state_ref[...] = jnp.zeros_like(state_ref)

    def conv_silu(h_ref, x_ref, w_ref):
        halo = jnp.where(c > 0, h_ref[0], 0.0)
        return _silu(_causal_conv(halo, x_ref[0], w_ref[...]))

    q_all = conv_silu(qh_ref, q_ref, cwq_ref)
    k_all = conv_silu(kh_ref, k_ref, cwk_ref)
    v_all = conv_silu(vh_ref, v_ref, cwv_ref)
    sm = sm_ref[0]
    lane = lax.broadcasted_iota(jnp.int32, sm.shape, 1)
    ri = lax.broadcasted_iota(jnp.int32, (rows, rows), 0)
    ci = lax.broadcasted_iota(jnp.int32, (rows, rows), 1)
    same_chunk = (ri // ch) == (ci // ch)
    incl = same_chunk & (ri >= ci)
    strict = same_chunk & (ri > ci)
    eye = (ri == ci).astype(F32)
    for j in range(heads_here):
        hd = pl.program_id(1) * heads_here + j
        hs = slice(j * dk, (j + 1) * dk)
        _dn_head(hd, q_all[:, hs], k_all[:, hs], v_all[:, hs], z_ref, sm, lane, smt_ref,
                 alog_ref, dtb_ref, nw_ref, o_ref, state_ref, j, hs, incl, strict, eye, n_heads)


def _dn_head(hd, q, k, v, z_ref, sm, lane, smt_ref, alog_ref, dtb_ref, nw_ref, o_ref, state_ref,
             j, hs, incl, strict, eye, n_heads):
    rows, dk = q.shape
    ch = DN_CHUNK
    q = _l2_normalize(q) * (dk ** -0.5)
    k = _l2_normalize(k)
    a_scale = -jnp.exp(jnp.full((1, 1), alog_ref[hd], F32))
    dtb = dtb_ref[hd]
    a_col = jnp.sum(jnp.where(lane == hd, sm, 0.0), axis=1, keepdims=True)
    b_col = jnp.sum(jnp.where(lane == hd + n_heads, sm, 0.0), axis=1, keepdims=True)
    beta = _sigmoid(b_col)
    gc_col = _segment_cumsum(a_scale * _softplus(a_col + dtb), 0, ch)
    a_row = smt_ref[pl.ds(hd, 1), :]
    gc_row = _segment_cumsum(a_scale * _softplus(a_row + dtb), 1, ch)

    n_ch = rows // ch
    decay = jnp.where(incl, jnp.exp(jnp.where(incl, gc_col - gc_row, 0.0)), 0.0)
    kb = k * beta
    kb16, k16 = kb.astype(BF16), k.astype(BF16)
    lower = jnp.where(strict, _dot_nt(kb16, k16) * decay, 0.0)
    p = -lower
    t_inv = eye + p
    m = 2
    while m < ch:
        p16 = p.astype(BF16)
        p = _dot(p16, p16)
        t_inv = t_inv + _dot(t_inv.astype(BF16), p.astype(BF16))
        m *= 2
    t16 = t_inv.astype(BF16)
    eg = jnp.exp(gc_col)
    u16 = _dot(t16, (v * beta).astype(BF16)).astype(BF16)
    w16 = _dot(t16, (kb * eg).astype(BF16)).astype(BF16)
    attn16 = (_dot_nt(q.astype(BF16), k16) * decay).astype(BF16)
    g_ends = [gc_col[(n + 1) * ch - 1:(n + 1) * ch] for n in range(n_ch)]
    g_end_rows = jnp.concatenate([jnp.broadcast_to(g, (ch, 1)) for g in g_ends], axis=0)
    kg16 = (k * jnp.exp(g_end_rows - gc_col)).astype(BF16)
    qp16 = (q * eg - _dot(attn16, w16)).astype(BF16)
    au = _dot(attn16, u16)
    kws, cs = [], []
    for n in range(n_ch):
        sl = slice(n * ch, (n + 1) * ch)
        kws.append(_dot_tn(kg16[sl], w16[sl]).astype(BF16))
        cs.append(_dot_tn(kg16[sl], u16[sl]))
    state = state_ref[j]
    outs = []
    for n in range(n_ch):
        sl = slice(n * ch, (n + 1) * ch)
        s16 = state.astype(BF16)
        outs.append(_dot(qp16[sl], s16) + au[sl])
        state = state * jnp.exp(g_ends[n]) - _dot(kws[n], s16) + cs[n]
    state_ref[j] = state
    o = jnp.concatenate(outs, axis=0) if len(outs) > 1 else outs[0]
    o = o * lax.rsqrt(jnp.mean(o * o, axis=-1, keepdims=True) + 1e-6) * nw_ref[...]
    o_ref[0, :, hs] = (o * _silu(z_ref[0, :, hs])).astype(o_ref.dtype)


DN_HEADS_PER_STEP = 4


def _deltanet(h3, small_t, conv_w, a_log, dt_bias, norm_w, *, n_heads, col0, small_col, rows=DN_ROWS):
    bsz, t_len, _ = h3.shape
    rows = min(rows, t_len)
    hps = DN_HEADS_PER_STEP
    dk = DN_HEAD_DIM
    wblk = hps * dk
    hb = rows // V7X_SUBLANES
    nblk = t_len // rows
    cb = col0 // wblk
    per = n_heads // hps
    smb = small_col // 256

    def halo(off):
        return pl.BlockSpec((1, V7X_SUBLANES, wblk),
                            lambda b, h, c: (b, jnp.maximum(c * hb - 1, 0), cb + off * per + h))

    def cur(off):
        return pl.BlockSpec((1, rows, wblk), lambda b, h, c: (b, c, cb + off * per + h))

    def convw(off):
        return pl.BlockSpec((DN_CONV, wblk), lambda b, h, c: (0, off * per + h))

    smem = pl.BlockSpec(memory_space=pltpu.SMEM)
    return pl.pallas_call(
        functools.partial(_dn_kernel, n_heads=n_heads),
        grid=(bsz, per, nblk),
        in_specs=[
            smem, smem,
            halo(0), cur(0), halo(1), cur(1), halo(2), cur(2), cur(3),
            pl.BlockSpec((1, rows, 256), lambda b, h, c: (b, c, smb)),
            pl.BlockSpec((small_t.shape[0], rows), lambda b, h, c: (0, b * nblk + c)),
            convw(0), convw(1), convw(2),
            pl.BlockSpec((1, dk), lambda b, h, c: (0, 0)),
        ],
        out_specs=pl.BlockSpec((1, rows, wblk), lambda b, h, c: (b, c, h)),
        out_shape=jax.ShapeDtypeStruct((bsz, t_len, n_heads * dk), BF16),
        scratch_shapes=[pltpu.VMEM((hps, dk, dk), F32)],
        compiler_params=_params(("arbitrary", "arbitrary", "arbitrary")),
        name="deltanet",
    )(a_log, dt_bias, h3, h3, h3, h3, h3, h3, h3, h3, small_t,
      conv_w, conv_w, conv_w, norm_w.reshape(1, dk))


NSA_QB = 128
NSA_TK = 512
NSA_MASK_BIG = -1e30
NSA_BIAS_TILES = -(-(REL_MAX_DIST + NSA_QB - 1) // NSA_QB) + 1


def _rel_bucket(dist):
    n = jnp.maximum(dist, 0)
    exact = REL_BUCKETS // 2
    big = exact + (jnp.log(jnp.maximum(n, 1).astype(jnp.float32) / exact)
                   / math.log(REL_MAX_DIST / exact) * (REL_BUCKETS - exact)).astype(jnp.int32)
    return jnp.where(n < exact, n, jnp.minimum(big, REL_BUCKETS - 1))


def _bias_kernel(rb_ref, bk_ref, o_ref):
    bk = bk_ref[...]
    for h in range(o_ref.shape[0]):
        acc = jnp.zeros(bk.shape, F32)
        for b in range(REL_BUCKETS):
            acc = jnp.where(bk == b, rb_ref[b, h], acc)
        o_ref[h] = acc


def _bias_table(rel_bias, buckets, rows=256):
    m, w = buckets.shape
    n_heads = rel_bias.shape[1]
    rows = min(rows, m)
    return pl.pallas_call(
        _bias_kernel,
        grid=(m // rows,),
        in_specs=[pl.BlockSpec(memory_space=pltpu.SMEM),
                  pl.BlockSpec((rows, w), lambda i: (i, 0))],
        out_specs=pl.BlockSpec((n_heads, rows, w), lambda i: (0, i, 0)),
        out_shape=jax.ShapeDtypeStruct((n_heads, m, w), F32),
        compiler_params=_params(("arbitrary",)),
        name="nsa_bias_table",
    )(rel_bias, buckets)


def _cmp_kernel(z_ref, pe_ref, w1_ref, w2_ref, o_ref):
    z = z_ref[0, 0]
    r, half = z.shape
    w1 = w1_ref[0]
    top = _dot(z, w1[:half])
    bot = _dot(z, w1[half:])
    bias = _dot(pe_ref[0], w1)[0:1]
    pre = top + pltpu.roll(bot, r - 1, 0) + bias
    o_ref[0, 0] = _dot(_gelu_tanh(pre).astype(BF16), w2_ref[0]).astype(o_ref.dtype)


def _nsa_compress(z, pe, w1, w2):
    bsz, _, r, half = z.shape
    d = w2.shape[-1]
    return pl.pallas_call(
        _cmp_kernel,
        grid=(bsz, 4),
        in_specs=[
            pl.BlockSpec((1, 1, r, half), lambda b, j: (b, j, 0, 0)),
            pl.BlockSpec((1, V7X_SUBLANES, 2 * half), lambda b, j: (j // 2, 0, 0)),
            pl.BlockSpec((1, 2 * half, w1.shape[-1]), lambda b, j: (j // 2, 0, 0)),
            pl.BlockSpec((1, w2.shape[1], d), lambda b, j: (j // 2, 0, 0)),
        ],
        out_specs=pl.BlockSpec((1, 1, r, d), lambda b, j: (b, j, 0, 0)),
        out_shape=jax.ShapeDtypeStruct((bsz, 4, r, d), BF16),
        compiler_params=_params(("arbitrary", "arbitrary")),
        name="nsa_compress",
    )(z, pe, w1, w2)


def _masked_exp(logits, mask):
    logits = jnp.where(mask, logits, NEG)
    m = jnp.max(logits, axis=-1, keepdims=True)
    e = jnp.where(mask, jnp.exp(logits - m), 0.0)
    return e, 1.0 / jnp.maximum(jnp.sum(e, axis=-1, keepdims=True), 1e-30)


def _nsa_kernel(q_ref, sm_ref, kc_ref, vc_ref, bc_ref, bt_ref, ovt_ref,
                ks_ref, vs_ref, kw_ref, vw_ref, o_ref, *, n_cmp, n_top, gate_col):
    g = pl.program_id(1)
    qb = pl.program_id(2)
    qr = NSA_QB
    d = NSA_HEAD_DIM
    hpg = q_ref.shape[2] // d
    s0 = qb * qr
    q = q_ref[0] * (d ** -0.5)
    q4 = jnp.concatenate([q[:, h * d:(h + 1) * d] for h in range(hpg)], axis=0).astype(BF16)
    t_col = s0 + lax.broadcasted_iota(jnp.int32, (qr, 1), 0)
    t4 = jnp.concatenate([t_col] * hpg, axis=0)

    r = kc_ref.shape[2]
    n_sel = ovt_ref.shape[0]
    sc = _dot_nt(q4, kc_ref[0, 0]) + bc_ref[...].reshape(hpg * qr, r)
    n_row = lax.broadcasted_iota(jnp.int32, (1, r), 1)
    cmp_end = n_row * NSA_CMP_STRIDE + (NSA_CMP_BLOCK - 1)
    e4, inv4 = _masked_exp(sc, (cmp_end <= t4) & (n_row < n_cmp))
    p4 = e4 * inv4
    o_cmp = _dot(p4.astype(BF16), vc_ref[0, 0])

    p_sum = p4[0:qr]
    for h in range(1, hpg):
        p_sum = p_sum + p4[h * qr:(h + 1) * qr]
    p_hi = p_sum.astype(BF16)
    p_lo = (p_sum - p_hi.astype(F32)).astype(BF16)
    ovt = ovt_ref[...]
    imp = _dot_nt(ovt, p_hi) + _dot_nt(ovt, p_lo)
    j_col = lax.broadcasted_iota(jnp.int32, (n_sel, 1), 0)
    t_row = s0 + lax.broadcasted_iota(jnp.int32, (1, qr), 1)
    q_blk = t_row // NSA_SEL_BLOCK
    forced = (j_col == 0) | (j_col == q_blk) | (j_col == q_blk - 1)
    score = jnp.where(j_col * NSA_SEL_BLOCK <= t_row,
                      jnp.where(forced, FORCE_SCORE, imp), -FORCE_SCORE)
    sel_t = (_rank_rows(score) < n_top).astype(BF16)

    tk = NSA_TK
    key_lane = lax.broadcasted_iota(jnp.int32, (1, tk), 1)
    nb = bt_ref.shape[0]
    ii = lax.broadcasted_iota(jnp.int32, (n_sel, n_sel), 0)
    jj = lax.broadcasted_iota(jnp.int32, (n_sel, n_sel), 1)
    not_sel = _dot_tn((1.0 - sel_t.astype(F32)).astype(BF16), (ii == jj).astype(BF16))
    q_aug = jnp.concatenate([q4, jnp.concatenate([not_sel.astype(BF16)] * hpg, axis=0)], axis=1)

    def sel_step(j, carry, causal):
        m_run, l_run, acc = carry
        k0 = pl.multiple_of(j * tk, tk)
        kt = ks_ref[0, 0, pl.ds(k0, tk), :]
        vt = vs_ref[0, 0, pl.ds(k0, tk), :]
        bias = jnp.concatenate(
            [bt_ref[jnp.clip(qb - (tk // qr) * j - mm, 0, nb - 1)].reshape(hpg * qr, qr)
             for mm in range(tk // qr)], axis=1)
        s = _dot_nt(q_aug, kt) + bias
        if causal:
            off = jnp.where(k0 + key_lane <= t_col, 0.0, NEG)
            s = s + jnp.concatenate([off] * hpg, axis=0)
        m_new = jnp.maximum(m_run, jnp.max(s, axis=-1, keepdims=True))
        scale = jnp.exp(m_run - m_new)
        e = jnp.exp(s - m_new)
        l_new = l_run * scale + jnp.sum(e, axis=-1, keepdims=True)
        acc = acc * scale + _dot(e.astype(BF16), vt)
        return m_new, l_new, acc

    init = (jnp.full((hpg * qr, 1), NEG, F32), jnp.zeros((hpg * qr, 1), F32),
            jnp.zeros((hpg * qr, d), F32))
    last = (s0 + qr - 1) // tk
    carry = lax.fori_loop(0, last, lambda j, c: sel_step(j, c, False), init)
    _, l_sel, acc_sel = sel_step(last, carry, True)
    o_sel = acc_sel / jnp.maximum(l_sel, 1e-30)

    n_wt = NSA_WINDOW // qr + 1
    kws, vws, biases, masks = [], [], [], []
    i_col = lax.broadcasted_iota(jnp.int32, (qr, 1), 0)
    j_row = lax.broadcasted_iota(jnp.int32, (1, qr), 1)
    for mm in range(n_wt):
        k0 = s0 - NSA_WINDOW + mm * qr
        k0c = pl.multiple_of(jnp.maximum(k0, 0), qr)
        kws.append(kw_ref[0, 0, pl.ds(k0c, qr), :])
        vws.append(vw_ref[0, 0, pl.ds(k0c, qr), :])
        biases.append(bt_ref[n_wt - 1 - mm].reshape(hpg * qr, qr))
        dist = (n_wt - 1 - mm) * qr + i_col - j_row
        masks.append((dist >= 0) & (dist < NSA_WINDOW) & (k0 >= 0))
    sw = _dot_nt(q4, jnp.concatenate(kws, axis=0)) + jnp.concatenate(biases, axis=1)
    mw = jnp.concatenate(masks, axis=1)
    ew, inv_w = _masked_exp(sw, jnp.concatenate([mw] * hpg, axis=0))
    o_win = _dot(ew.astype(BF16), jnp.concatenate(vws, axis=0)) * inv_w

    sm = sm_ref[0]
    lane = lax.broadcasted_iota(jnp.int32, sm.shape, 1)

    def gate(h, branch):
        col = gate_col + 3 * (g * hpg + h) + branch
        return _sigmoid(jnp.sum(jnp.where(lane == col, sm, 0.0), axis=1, keepdims=True))

    outs = []
    for h in range(hpg):
        sl = slice(h * qr, (h + 1) * qr)
        outs.append(gate(h, 0) * o_cmp[sl] + gate(h, 1) * o_sel[sl] + gate(h, 2) * o_win[sl])
    o_ref[0] = jnp.concatenate(outs, axis=1).astype(o_ref.dtype)


def _nsa_attention(h3, kv_t, ks_aug, cmp_kv, bias_c, bias_t, ovt, *, q_col, small_col, gate_col, n_cmp):
    bsz, t_len, _ = h3.shape
    g_n = NSA_KV_GROUPS
    d = NSA_HEAD_DIM
    n_heads = bias_c.shape[0]
    hpg = n_heads // g_n
    qw = hpg * d
    r = cmp_kv.shape[2]
    n_sel = t_len // NSA_SEL_BLOCK
    nb = bias_t.shape[0]
    kv_spec = lambda part: pl.BlockSpec((1, 1, t_len, d), lambda b, g, q: (b, 2 * part + g, 0, 0))
    return pl.pallas_call(
        functools.partial(_nsa_kernel, n_cmp=n_cmp, n_top=min(NSA_TOP_BLOCKS, n_sel),
                          gate_col=gate_col),
        grid=(bsz, g_n, t_len // NSA_QB),
        in_specs=[
            pl.BlockSpec((1, NSA_QB, qw), lambda b, g, q: (b, q, q_col // qw + g)),
            pl.BlockSpec((1, NSA_QB, 256), lambda b, g, q: (b, q, small_col // 256)),
            pl.BlockSpec((1, 1, r, d), lambda b, g, q: (b, g, 0, 0)),
            pl.BlockSpec((1, 1, r, d), lambda b, g, q: (b, g_n + g, 0, 0)),
            pl.BlockSpec((hpg, NSA_QB, r), lambda b, g, q: (g, q, 0)),
            pl.BlockSpec((nb, hpg, NSA_QB, NSA_QB), lambda b, g, q: (0, g, 0, 0)),
            pl.BlockSpec((n_sel, r), lambda b, g, q: (0, 0)),
            pl.BlockSpec((1, 1, t_len, ks_aug.shape[3]), lambda b, g, q: (b, g, 0, 0)),
            kv_spec(3), kv_spec(4), kv_spec(5),
        ],
        out_specs=pl.BlockSpec((1, NSA_QB, qw), lambda b, g, q: (b, q, g)),
        out_shape=jax.ShapeDtypeStruct((bsz, t_len, n_heads * d), BF16),
        compiler_params=_params(("arbitrary", "arbitrary", "arbitrary")),
        name="nsa_attention",
    )(h3, h3, cmp_kv, cmp_kv, bias_c, bias_t, ovt, ks_aug, kv_t, kv_t, kv_t)


def _outproj_kernel(x_ref, ya_ref, yb_ref, yc_ref, wa_ref, wb_ref, wc_ref, g_ref, b_ref, o_ref,
                    *, alpha):
    mix = _dot(ya_ref[...], wa_ref[...]) + _dot(yb_ref[...], wb_ref[...]) + _dot(yc_ref[...], wc_ref[...])
    o_ref[...] = _layer_norm(alpha * x_ref[...] + mix, g_ref[...], b_ref[...])


def _outproj_ln(x2d, ya, yb, yc, w_out, g, b, alpha, tm=256):
    n, d = x2d.shape
    wa_n, wb_n, wc_n = ya.shape[1], yb.shape[1], yc.shape[1]
    w16 = w_out.astype(BF16)
    row = lambda w: pl.BlockSpec((tm, w), lambda i: (i, 0))
    full = lambda r: pl.BlockSpec((r, d), lambda i: (0, 0))
    return pl.pallas_call(
        functools.partial(_outproj_kernel, alpha=alpha),
        grid=(n // tm,),
        in_specs=[row(d), row(wa_n), row(wb_n), row(wc_n),
                  full(wa_n), full(wb_n), full(wc_n), full(1), full(1)],
        out_specs=row(d),
        out_shape=jax.ShapeDtypeStruct((n, d), F32),
        compiler_params=_params(("arbitrary",)),
        name="outproj_ln",
    )(x2d, ya, yb, yc, w16[:wa_n], w16[wa_n:wa_n + wb_n], w16[wa_n + wb_n:],
      g.reshape(1, d), b.reshape(1, d))


def _mm_kernel(x_ref, w_ref, o_ref):
    o_ref[...] = _dot(x_ref[...].astype(BF16), w_ref[...]).astype(o_ref.dtype)


def _matmul(x2d, w16, out_dtype, tm=256):
    n, d = x2d.shape
    cols = w16.shape[1]
    tm = min(tm, n)
    return pl.pallas_call(
        _mm_kernel,
        grid=(n // tm,),
        in_specs=[pl.BlockSpec((tm, d), lambda i: (i, 0)),
                  pl.BlockSpec((d, cols), lambda i: (0, 0))],
        out_specs=pl.BlockSpec((tm, cols), lambda i: (i, 0)),
        out_shape=jax.ShapeDtypeStruct((n, cols), out_dtype),
        compiler_params=_params(("arbitrary",)),
        name="mem_kv_proj",
    )(x2d, w16)


def _xattn_kernel(x_ref, kv_ref, wq_ref, wo_ref, g_ref, b_ref, o_ref, o16_ref, oslab_ref, *, alpha):
    x = x_ref[...]
    q = _dot(x.astype(BF16), wq_ref[...])
    kv = kv_ref[0]
    width = wq_ref.shape[1]
    hd = XA_HEAD_DIM
    outs = []
    for h in range(width // hd):
        qh = q[:, h * hd:(h + 1) * hd].astype(BF16)
        kh = kv[:, h * hd:(h + 1) * hd]
        vh = kv[:, width + h * hd:width + (h + 1) * hd]
        logits = _dot_nt(qh, kh) * (hd ** -0.5)
        m = jnp.max(logits, axis=-1, keepdims=True)
        e = jnp.exp(logits - m)
        p = e / jnp.sum(e, axis=-1, keepdims=True)
        outs.append(_dot(p.astype(BF16), vh))
    o = jnp.concatenate(outs, axis=1).astype(BF16)
    y = _layer_norm(alpha * x + _dot(o, wo_ref[...]), g_ref[...], b_ref[...])
    o_ref[...] = y
    o16_ref[...] = y.astype(BF16)
    tm = y.shape[0]
    cpr = y.shape[1] // V7X_LANES
    for c in range(cpr):
        oslab_ref[pl.ds(c, tm, stride=cpr), :] = y[:, c * V7X_LANES:(c + 1) * V7X_LANES]


def _cross_attention_ln(x2d, kv, wq, wo, g, b, alpha, t_len, tm=256):
    n, d = x2d.shape
    width = wq.shape[1]
    m_len = kv.shape[1]
    per_b = t_len // tm
    cpr = d // V7X_LANES
    return pl.pallas_call(
        functools.partial(_xattn_kernel, alpha=alpha),
        grid=(n // tm,),
        in_specs=[
            pl.BlockSpec((tm, d), lambda i: (i, 0)),
            pl.BlockSpec((1, m_len, 2 * width), lambda i: (i // per_b, 0, 0)),
            pl.BlockSpec((d, width), lambda i: (0, 0)),
            pl.BlockSpec((width, d), lambda i: (0, 0)),
            pl.BlockSpec((1, d), lambda i: (0, 0)),
            pl.BlockSpec((1, d), lambda i: (0, 0)),
        ],
        out_specs=[pl.BlockSpec((tm, d), lambda i: (i, 0)), pl.BlockSpec((tm, d), lambda i: (i, 0)),
                   pl.BlockSpec((tm * cpr, V7X_LANES), lambda i: (i, 0))],
        out_shape=[jax.ShapeDtypeStruct((n, d), F32), jax.ShapeDtypeStruct((n, d), BF16),
                   jax.ShapeDtypeStruct((n * cpr, V7X_LANES), F32)],
        compiler_params=_params(("arbitrary",)),
        name="cross_attention_ln",
    )(x2d, kv, wq.astype(BF16), wo.astype(BF16), g.reshape(1, d), b.reshape(1, d))


def _rank_rows(v):
    sl = V7X_SUBLANES
    sub = lax.broadcasted_iota(jnp.int32, (sl, 1), 0)
    tiles = [v[t * sl:(t + 1) * sl] for t in range(v.shape[0] // sl)]
    ranks = [jnp.zeros(tile.shape, jnp.int32) for tile in tiles]
    for k in range(v.shape[0]):
        row = v[k:k + 1]
        for t, tile in enumerate(tiles):
            if t < k // sl:
                ahead = row > tile
            elif t > k // sl:
                ahead = row >= tile
            else:
                ahead = (row > tile) | ((row == tile) & (sub > k % sl))
            ranks[t] = ranks[t] + ahead.astype(jnp.int32)
    return jnp.concatenate(ranks, axis=0)


def _router_kernel(x_ref, w_ref, b_ref, e_ref, g_ref, cnt_ref):
    n_exp = w_ref.shape[0]
    gsz = n_exp // N_GROUPS
    logits = lax.dot_general(w_ref[...], x_ref[...], (((1,), (1,)), ((), ())),
                             preferred_element_type=F32, precision=lax.Precision.HIGHEST)
    scores = _sigmoid(logits)
    choice = scores + b_ref[...]
    tm = scores.shape[1]
    sub = lax.broadcasted_iota(jnp.int32, (gsz, 1), 0)
    grp_rows = []
    for gi in range(N_GROUPS):
        slab = choice[gi * gsz:(gi + 1) * gsz]
        m1 = jnp.max(slab, axis=0, keepdims=True)
        first = jnp.min(jnp.where(slab == m1, sub, gsz), axis=0, keepdims=True)
        m2 = jnp.max(jnp.where(sub == first, NEG, slab), axis=0, keepdims=True)
        grp_rows.append(m1 + m2)
    grp = jnp.concatenate(grp_rows, axis=0)
    grp_ok = _rank_rows(grp) < TOPK_GROUPS
    ok = jnp.concatenate([jnp.broadcast_to(grp_ok[gi:gi + 1], (gsz, tm)) for gi in range(N_GROUPS)], axis=0)
    e_idx = lax.broadcasted_iota(jnp.int32, (n_exp, 1), 0)
    rank = _rank_rows(jnp.where(ok, choice, -1e9))
    denom = jnp.sum(jnp.where(rank < TOP_K, scores, 0.0), axis=0, keepdims=True)
    e_rows, g_rows = [], []
    for k in range(TOP_K):
        hit = rank == k
        e_rows.append(jnp.sum(jnp.where(hit, e_idx, 0), axis=0, keepdims=True))
        g_rows.append(jnp.sum(jnp.where(hit, scores, 0.0), axis=0, keepdims=True) / denom * ROUTED_SCALE)
    e_ref[...] = jnp.concatenate(e_rows, axis=0)
    g_ref[...] = jnp.concatenate(g_rows, axis=0)

    @pl.when(pl.program_id(0) == 0)
    def _():
        cnt_ref[...] = jnp.zeros_like(cnt_ref)

    cnt_ref[...] += jnp.sum((rank < TOP_K).astype(jnp.int32), axis=1, keepdims=True)


def _router(x2d, w_router, bias, tm=512):
    n, d = x2d.shape
    n_exp = w_router.shape[1]
    return pl.pallas_call(
        _router_kernel,
        grid=(n // tm,),
        in_specs=[pl.BlockSpec((tm, d), lambda i: (i, 0)),
                  pl.BlockSpec((n_exp, d), lambda i: (0, 0)),
                  pl.BlockSpec((n_exp, 1), lambda i: (0, 0))],
        out_specs=[pl.BlockSpec((TOP_K, tm), lambda i: (0, i)), pl.BlockSpec((TOP_K, tm), lambda i: (0, i)),
                   pl.BlockSpec((n_exp, 1), lambda i: (0, 0))],
        out_shape=[jax.ShapeDtypeStruct((TOP_K, n), jnp.int32), jax.ShapeDtypeStruct((TOP_K, n), F32),
                   jax.ShapeDtypeStruct((n_exp, 1), jnp.int32)],
        compiler_params=_params(("arbitrary",)),
        name="moe_router",
    )(x2d, w_router.T, bias.reshape(n_exp, 1))


def _expert_kernel(e_ref, b_ref, lo_ref, hi_ref, first_ref,
                   tok_ref, dst_ref, tok_next_ref, x_hbm, wt_ref, wg_ref, wu_ref, wd_ref, y_hbm,
                   xbuf, ostage, obuf, wg16, wu16, wd16, gsem, ssem):
    i = pl.program_id(0)
    n_items = pl.num_programs(0)
    slot = i % 2
    rows = wt_ref.shape[0]
    cpr = ostage.shape[0] // rows
    xpitch = xbuf.shape[1] // rows
    lo, hi = lo_ref[i], hi_ref[i]

    def slab(ref, r, pitch=cpr):
        return ref.at[pl.ds(pl.multiple_of(r * pitch, math.gcd(pitch, cpr)), cpr)]

    def for_rows(r0, r1, start_row, unroll):
        n_groups = (r1 - r0) // unroll

        def group(g, _):
            base = r0 + g * unroll
            for u in range(unroll):
                start_row(base + u, u % 2)
            return 0

        def single(r, _):
            start_row(r, 0)
            return 0

        lax.fori_loop(0, n_groups, group, 0)
        lax.fori_loop(r0 + n_groups * unroll, r1, single, 0)

    def start_gather(idx_ref, r0, r1, s):
        def start_row(r, priority):
            pltpu.make_async_copy(slab(x_hbm, idx_ref[0, 0, r]), slab(xbuf.at[s], r, xpitch),
                                  gsem.at[s]).start(priority=priority)

        for_rows(r0, r1, start_row, MOE_GATHER_UNROLL)

    def wait_gather(n, s):
        @pl.when(n > 0)
        def _():
            pltpu.make_async_copy(x_hbm.at[pl.ds(0, n * cpr)], xbuf.at[s, pl.ds(0, n * cpr)],
                                  gsem.at[s]).wait()

    def wait_scatter(n, s):
        @pl.when(n > 0)
        def _():
            pltpu.make_async_copy(obuf.at[s, pl.ds(0, n * cpr)], y_hbm.at[pl.ds(0, n * cpr)],
                                  ssem.at[s]).wait()

    @pl.when(i == 0)
    def _():
        xbuf[...] = jnp.zeros_like(xbuf)
        start_gather(tok_ref, lo, hi, 0)

    @pl.when(i + 1 < n_items)
    def _():
        start_gather(tok_next_ref, lo_ref[i + 1], hi_ref[i + 1], 1 - slot)

    @pl.when(first_ref[i] == 1)
    def _():
        wg16[...] = wg_ref[...].astype(BF16)
        wu16[...] = wu_ref[...].astype(BF16)
        wd16[...] = wd_ref[...].astype(BF16)

    wait_gather(hi - lo, slot)

    @pl.when(i >= 2)
    def _():
        wait_scatter(hi_ref[i - 2] - lo_ref[i - 2], slot)

    @pl.when(hi > lo)
    def _():
        xb = jnp.concatenate([xbuf[slot, pl.ds(c, rows, stride=xpitch), :] for c in range(cpr)], axis=1)
        xb = xb.astype(BF16)
        hid = _silu(_dot(xb, wg16[...])) * _dot(xb, wu16[...])
        out = _dot(hid.astype(BF16), wd16[...]) * wt_ref[...]
        for c in range(cpr):
            ostage[pl.ds(c, rows, stride=cpr), :] = out[:, c * V7X_LANES:(c + 1) * V7X_LANES]
        obuf[slot] = ostage[...].astype(obuf.dtype)

        def start_row(r, priority):
            pltpu.make_async_copy(slab(obuf.at[slot], r), slab(y_hbm, dst_ref[0, 0, r]),
                                  ssem.at[slot]).start(priority=priority)

        for_rows(lo, hi, start_row, MOE_SCATTER_UNROLL)

    @pl.when(i == n_items - 1)
    def _():
        @pl.when(i >= 1)
        def _():
            wait_scatter(hi_ref[i - 1] - lo_ref[i - 1], 1 - slot)

        wait_scatter(hi - lo, slot)


def _experts(x_slab, plan, w_gate, w_up, w_down, layer):
    item_e, item_b, item_lo, item_hi, item_first, row_tok, row_dst, row_w = plan
    d = w_gate.shape[2]
    n = x_slab.shape[0] * V7X_LANES // d
    n_items = item_e.shape[0]
    dff = w_gate.shape[3]
    rows = MOE_ROWS
    n_blk = row_tok.shape[0] // rows
    cpr = d // V7X_LANES
    wspec = lambda r, c: pl.BlockSpec((None, None, r, c), lambda i, e, *_: (layer, e[i], 0, 0))
    cur = lambda i, e, b, *_: (b[i], 0, 0)
    nxt = lambda i, e, b, *_: (b[jnp.minimum(i + 1, n_items - 1)], 0, 0)
    grid_spec = pltpu.PrefetchScalarGridSpec(
        num_scalar_prefetch=5,
        grid=(n_items,),
        in_specs=[
            pl.BlockSpec((1, 1, rows), cur, memory_space=pltpu.SMEM),
            pl.BlockSpec((1, 1, rows), cur, memory_space=pltpu.SMEM),
            pl.BlockSpec((1, 1, rows), nxt, memory_space=pltpu.SMEM),
            pl.BlockSpec(memory_space=pl.ANY),
            pl.BlockSpec((rows, 1), lambda i, e, b, *_: (b[i], 0)),
            wspec(d, dff), wspec(d, dff), wspec(dff, d),
        ],
        out_specs=pl.BlockSpec(memory_space=pl.ANY),
        scratch_shapes=[
            pltpu.VMEM((2, rows * MOE_XBUF_PITCH, V7X_LANES), F32), pltpu.VMEM((rows * cpr, V7X_LANES), F32),
            pltpu.VMEM((2, rows * cpr, V7X_LANES), BF16),
            pltpu.VMEM((d, dff), BF16), pltpu.VMEM((d, dff), BF16), pltpu.VMEM((dff, d), BF16),
            pltpu.SemaphoreType.DMA((2,)), pltpu.SemaphoreType.DMA((2,)),
        ],
    )
    tok3 = row_tok.reshape(n_blk, 1, rows)
    return pl.pallas_call(
        _expert_kernel,
        grid_spec=grid_spec,
        out_shape=jax.ShapeDtypeStruct((n * TOP_K * cpr, V7X_LANES), BF16),
        compiler_params=_params(("arbitrary",)),
        name="moe_experts",
    )(item_e, item_b, item_lo, item_hi, item_first,
      tok3, row_dst.reshape(n_blk, 1, rows), tok3,
      x_slab, row_w.reshape(n_blk * rows, 1), w_gate, w_up, w_down)


def _combine_kernel(x_ref, x16_ref, y_ref, wg_ref, wu_ref, wd_ref, g_ref, b_ref, o_ref, acc_ref,
                    *, alpha):
    x16 = x16_ref[...]
    shared = _dot((_silu(_dot(x16, wg_ref[...])) * _dot(x16, wu_ref[...])).astype(BF16), wd_ref[...])
    tm, d = x_ref.shape
    cpr = d // V7X_LANES
    tot = y_ref[0].astype(F32)
    for k in range(1, TOP_K):
        tot = tot + y_ref[k].astype(F32)
    acc_ref[...] = tot
    routed = jnp.concatenate([acc_ref[pl.ds(c, tm, stride=cpr), :] for c in range(cpr)], axis=1)
    o_ref[...] = _layer_norm(alpha * x_ref[...] + (routed + shared), g_ref[...], b_ref[...])


def _moe_combine_ln(x2d, x16, y8, ws_gate, ws_up, ws_down, g, b, alpha, tm=128):
    n, d = x2d.shape
    dff = ws_gate.shape[1]
    cpr = d // V7X_LANES
    row = lambda w: pl.BlockSpec((tm, w), lambda i: (i, 0))
    full = lambda r, c: pl.BlockSpec((r, c), lambda i: (0, 0))
    return pl.pallas_call(
        functools.partial(_combine_kernel, alpha=alpha),
        grid=(n // tm,),
        in_specs=[row(d), row(d), pl.BlockSpec((TOP_K, tm * cpr, V7X_LANES), lambda i: (0, i, 0)),
                  full(d, dff), full(d, dff), full(dff, d), full(1, d), full(1, d)],
        out_specs=row(d),
        out_shape=jax.ShapeDtypeStruct((n, d), F32),
        scratch_shapes=[pltpu.VMEM((tm * cpr, V7X_LANES), F32)],
        compiler_params=_params(("arbitrary",)),
        name="moe_combine_ln",
    )(x2d, x16, y8.reshape(TOP_K, n * cpr, V7X_LANES), ws_gate.astype(BF16), ws_up.astype(BF16),
      ws_down.astype(BF16), g.reshape(1, d), b.reshape(1, d))


def _dispatch_plan(top_e, gate, counts):
    n = top_e.shape[1]
    n_exp = counts.shape[0]
    rows = MOE_ROWS
    n_assign = n * TOP_K
    n_items = n_assign // rows + n_exp
    a_iota = jnp.arange(n_assign, dtype=jnp.int32)
    _, row_dst, row_w = lax.sort((top_e.reshape(-1), a_iota, gate.reshape(-1)), num_keys=1)
    row_tok = row_dst % n
    ends = jnp.cumsum(counts)
    starts = ends - counts
    n_blk_e = jnp.where(counts > 0, (ends - 1) // rows - starts // rows + 1, 0)
    item_end = jnp.cumsum(n_blk_e)
    item_start = item_end - n_blk_e
    t = jnp.arange(n_items, dtype=jnp.int32)
    active = t < item_end[-1]
    item_e = jnp.minimum(jnp.sum((item_end[None, :] <= t[:, None]).astype(jnp.int32), axis=1), n_exp - 1)
    onehot = item_e[:, None] == jnp.arange(n_exp, dtype=jnp.int32)[None, :]
    pick = lambda v: jnp.sum(jnp.where(onehot, v[None, :], 0), axis=1)
    s_e, e_e, i_e = pick(starts), pick(ends), pick(item_start)
    item_b = jnp.clip(s_e // rows + (t - i_e), 0, n_assign // rows - 1)
    item_lo = jnp.where(active, jnp.clip(s_e - item_b * rows, 0, rows), 0)
    item_hi = jnp.where(active, jnp.clip(e_e - item_b * rows, 0, rows), 0)
    item_first = (active & (t == i_e)).astype(jnp.int32)
    i32 = lambda v: v.astype(jnp.int32)
    return (i32(item_e), i32(item_b), i32(item_lo), i32(item_hi), item_first,
            i32(row_tok), i32(row_dst), row_w)


def kernel(x, mem, rel_bias, w_in, rg_conv_w, rg_conv_b, rg_wa, rg_ba, rg_wi, rg_bi, rg_lambda,
           dn_conv_w, dn_a_log, dn_dt_bias, dn_norm_w, nsa_pe_k, nsa_pe_v, nsa_phi_k1, nsa_phi_k2,
           nsa_phi_v1, nsa_phi_v2, w_out, ln1_g, ln1_b, xa_wq, xa_wk, xa_wv, xa_wo, ln2_g, ln2_b,
           moe_router, moe_router_bias, moe_w_gate, moe_w_up, moe_w_down, shared_w_gate, shared_w_up,
           shared_w_down, ln3_g, ln3_b):
    bsz, t_len, d_model = x.shape
    depth = w_in.shape[0]
    n_tok = bsz * t_len
    alpha = (2 * depth) ** 0.25
    rg_w = rg_conv_w.shape[2]
    dn_w = dn_conv_w.shape[2] // 3
    dn_heads = dn_a_log.shape[1]
    nsa_heads = rel_bias.shape[1]
    nsa_w = nsa_heads * NSA_HEAD_DIM
    kv_w = NSA_KV_GROUPS * NSA_HEAD_DIM
    n_exp = moe_router.shape[2]

    sizes = (rg_w, rg_w, dn_w, dn_w, dn_w, dn_w, dn_heads, dn_heads, nsa_w, 6 * kv_w, 3 * nsa_heads)
    offs = [0]
    for s in sizes:
        offs.append(offs[-1] + s)
    col = lambda w, i: w[:, :, offs[i]:offs[i + 1]]
    main = offs[6]
    q_col = main
    small_col = q_col + nsa_w
    kv_col = small_col + 256
    gate_col = 2 * dn_heads
    small = jnp.concatenate([col(w_in, 6), col(w_in, 7), col(w_in, 10)], axis=2)
    small = jnp.pad(small, ((0, 0), (0, 0), (0, 256 - small.shape[2])))
    w_slab = jnp.concatenate([w_in[:, :, :main], col(w_in, 8), small, col(w_in, 9)], axis=2).astype(BF16)
    w_small_t = jnp.swapaxes(jnp.concatenate([col(w_in, 6), col(w_in, 7)], axis=2), 1, 2).astype(BF16)

    r_cmp = t_len // NSA_CMP_STRIDE
    n_cmp = (t_len - NSA_CMP_BLOCK) // NSA_CMP_STRIDE + 1
    n_sel = t_len // NSA_SEL_BLOCK
    cmp_start = jnp.arange(r_cmp) * NSA_CMP_STRIDE
    cmp_end = cmp_start + NSA_CMP_BLOCK - 1
    sel_start = jnp.arange(n_sel) * NSA_SEL_BLOCK
    ovt = ((cmp_start[None, :] < sel_start[:, None] + NSA_SEL_BLOCK)
           & (cmp_end[None, :] >= sel_start[:, None])
           & (jnp.arange(r_cmp)[None, :] < n_cmp)).astype(BF16)
    blk_ind = jnp.where(jnp.arange(t_len)[:, None] // NSA_SEL_BLOCK == jnp.arange(n_sel)[None, :],
                        NSA_MASK_BIG, 0.0).astype(BF16)
    bucket_c = _rel_bucket(jnp.arange(t_len)[:, None] - cmp_end[None, :]).astype(jnp.int32)
    tile = jnp.arange(NSA_BIAS_TILES * NSA_QB)[:, None] - jnp.arange(NSA_QB)[None, :]
    bucket_t = _rel_bucket(tile).astype(jnp.int32)
    bias_c = _bias_table(rel_bias, bucket_c)
    bias_t = _bias_table(rel_bias, bucket_t).reshape(nsa_heads, NSA_BIAS_TILES, NSA_QB, NSA_QB)
    bias_t = jnp.swapaxes(bias_t, 0, 1)

    cmp_in = NSA_CMP_BLOCK * NSA_HEAD_DIM
    x2d = x.reshape(n_tok, d_model)
    mem2d = mem.reshape(-1, d_model)
    for l in range(depth):
        h, small_t = _inproj(x2d, w_slab, w_small_t, l)
        h3 = h.reshape(bsz, t_len, -1)
        y_a = _rglru(h3, rg_conv_w[l], rg_conv_b[l], rg_wa[l], rg_wi[l], rg_ba[l], rg_bi[l],
                     rg_lambda[l], rg_w)
        y_b = _deltanet(h3, small_t, dn_conv_w[l], dn_a_log[l], dn_dt_bias[l], dn_norm_w[l],
                        n_heads=dn_heads, col0=2 * rg_w, small_col=small_col)
        kv_t = h3[:, :, kv_col:].reshape(bsz, t_len, 6 * NSA_KV_GROUPS, NSA_HEAD_DIM)
        kv_t = jnp.swapaxes(kv_t, 1, 2).astype(BF16)
        z = kv_t[:, :2 * NSA_KV_GROUPS].reshape(bsz, 2 * NSA_KV_GROUPS, r_cmp, cmp_in // 2)
        pe = jnp.stack([nsa_pe_k[l], nsa_pe_v[l]]).reshape(2, 1, cmp_in)
        pe = jnp.broadcast_to(pe, (2, V7X_SUBLANES, cmp_in)).astype(BF16)
        w1 = jnp.stack([nsa_phi_k1[l], nsa_phi_v1[l]]).astype(BF16)
        w2 = jnp.stack([nsa_phi_k2[l], nsa_phi_v2[l]]).astype(BF16)
        cmp_kv = _nsa_compress(z, pe, w1, w2)
        ks_aug = jnp.concatenate(
            [kv_t[:, 2 * NSA_KV_GROUPS:3 * NSA_KV_GROUPS],
             jnp.broadcast_to(blk_ind, (bsz, NSA_KV_GROUPS, t_len, n_sel))], axis=-1)
        y_c = _nsa_attention(h3, kv_t, ks_aug, cmp_kv, bias_c, bias_t, ovt, q_col=q_col,
                             small_col=small_col, gate_col=gate_col, n_cmp=n_cmp)
        x2d = _outproj_ln(x2d, y_a.reshape(n_tok, -1), y_b.reshape(n_tok, -1), y_c.reshape(n_tok, -1),
                          w_out[l], ln1_g[l], ln1_b[l], alpha)
        w_kv = jnp.concatenate([xa_wk[l], xa_wv[l]], axis=1).astype(BF16)
        kv_mem = _matmul(mem2d, w_kv, BF16).reshape(bsz, mem.shape[1], -1)
        x2d, x16, x_slab = _cross_attention_ln(x2d, kv_mem, xa_wq[l], xa_wo[l], ln2_g[l], ln2_b[l],
                                               alpha, t_len)
        top_e, gate, counts = _router(x2d, moe_router[l], moe_router_bias[l])
        plan = _dispatch_plan(top_e, gate, counts.reshape(-1))
        y8 = _experts(x_slab, plan, moe_w_gate, moe_w_up, moe_w_down, l)
        x2d = _moe_combine_ln(x2d, x16, y8, shared_w_gate[l], shared_w_up[l], shared_w_down[l],
                              ln3_g[l], ln3_b[l], alpha)
    return x2d.reshape(bsz, t_len, d_model)
```
